```python
import jax, jax.numpy as jnp
from jax import lax
import numpy as np

D_MODEL = 1024
BATCH = 16
SEQ = 256
DEPTH = 1
DEC_BATCH = 4
DEC_SEQ = 4096
PAST_LEN = 256

GRID_W = 64
MIX_WIDTH = D_MODEL
MLSTM_WIDTH = MIX_WIDTH // 2
FOURIER_WIDTH = MIX_WIDTH - MLSTM_WIDTH
MLSTM_HEADS = 4
HEAD_DIM = MLSTM_WIDTH // MLSTM_HEADS
FOURIER_GROUPS = 4
FOURIER_GROUP_DIM = FOURIER_WIDTH // FOURIER_GROUPS
CHUNK = 128
N_GATES = 4 * MLSTM_HEADS
OFF_Q = 0
OFF_K = MLSTM_WIDTH
OFF_V = 2 * MLSTM_WIDTH
OFF_O = 3 * MLSTM_WIDTH
OFF_G = 4 * MLSTM_WIDTH
OFF_F = OFF_G + N_GATES
IN_COLS = OFF_F + FOURIER_WIDTH
N_EXPERTS = 16
CAPACITY_FACTOR = 2
D_EXPERT = 2 * D_MODEL
N_MOD = 6
EPS = 1e-6

kernel_name = "hymba_mlstm_fnet_ec_moe_diffusion_step"


def rmsnorm(x, g):
    xf = x.astype(jnp.float32)
    y = xf * lax.rsqrt(jnp.mean(xf * xf, axis=-1, keepdims=True) + EPS)
    return (y * g).astype(x.dtype)


def mlstm_chunkwise(q, k, v, i_pre, f_pre, C0, n0, m0):
    B, H, T, dh = q.shape
    L = min(CHUNK, T)
    N = T // L
    q = q.reshape(B, H, N, L, dh)
    k = k.reshape(B, H, N, L, dh)
    v = v.reshape(B, H, N, L, dh)
    ig = i_pre.reshape(B, H, N, L)
    b = jnp.cumsum(jax.nn.log_sigmoid(f_pre).reshape(B, H, N, L), axis=-1)
    g = b[..., -1]
    w_log = g[..., None] - b + ig
    m_loc = jnp.max(w_log, axis=-1)
    w = jnp.exp(w_log - m_loc[..., None])
    C_loc = jnp.einsum("bhnl,bhnlv,bhnlk->bhnvk", w, v, k)
    n_loc = jnp.einsum("bhnl,bhnlk->bhnk", w, k)

    def step(carry, inp):
        C, n, m = carry
        g_c, m_l, C_l, n_l = inp
        m_new = jnp.maximum(g_c + m, m_l)
        a = jnp.exp(g_c + m - m_new)
        bb = jnp.exp(m_l - m_new)
        C_new = a[..., None, None] * C + bb[..., None, None] * C_l
        n_new = a[..., None] * n + bb[..., None] * n_l
        return (C_new, n_new, m_new), (C, n, m)

    xs = (jnp.moveaxis(g, 2, 0), jnp.moveaxis(m_loc, 2, 0),
          jnp.moveaxis(C_loc, 2, 0), jnp.moveaxis(n_loc, 2, 0))
    (Cf, nf, mf), (Cs, ns, ms) = lax.scan(step, (C0, n0, m0), xs)
    Cs = jnp.moveaxis(Cs, 0, 2)
    ns = jnp.moveaxis(ns, 0, 2)
    ms = jnp.moveaxis(ms, 0, 2)

    inter_log = b + ms[..., None]
    causal = jnp.tril(jnp.ones((L, L), dtype=bool))
    D = jnp.where(causal, b[..., :, None] - b[..., None, :] + ig[..., None, :], -jnp.inf)
    m_comb = jnp.maximum(inter_log, jnp.max(D, axis=-1))
    S = jnp.einsum("bhnld,bhnsd->bhnls", q, k) * jnp.exp(D - m_comb[..., None])
    w_inter = jnp.exp(inter_log - m_comb)
    num = (jnp.einsum("bhnls,bhnsd->bhnld", S, v)
           + w_inter[..., None] * jnp.einsum("bhnvk,bhnlk->bhnlv", Cs, q))
    den = S.sum(-1) + w_inter * jnp.einsum("bhnk,bhnlk->bhnl", ns, q)
    h = num / jnp.maximum(jnp.abs(den), jnp.exp(-m_comb))[..., None]
    return h.reshape(B, H, T, dh), (Cf, nf, mf)


def mixer(h, w_in, b_in, head_g, w_out, init, grid_w):
    B, T, _ = h.shape
    p = (h @ w_in + b_in).astype(jnp.float32)

    def heads(x):
        return x.reshape(B, T, MLSTM_HEADS, HEAD_DIM).transpose(0, 2, 1, 3)

    q = heads(p[..., OFF_Q:OFF_K])
    k = heads(p[..., OFF_K:OFF_V]) * (HEAD_DIM ** -0.5)
    v = heads(p[..., OFF_V:OFF_O])
    o = p[..., OFF_O:OFF_G]
    gates = p[..., OFF_G:OFF_F].reshape(B, T, 4, MLSTM_HEADS).transpose(2, 0, 3, 1)
    if init is None:
        C0 = jnp.zeros((B, 2, MLSTM_HEADS, HEAD_DIM, HEAD_DIM), jnp.float32)
        n0 = jnp.zeros((B, 2, MLSTM_HEADS, HEAD_DIM), jnp.float32)
        m0 = jnp.zeros((B, 2, MLSTM_HEADS), jnp.float32)
    else:
        C0, n0, m0 = (s.astype(jnp.float32) for s in init)
    h_f, (Cf, nf, mf) = mlstm_chunkwise(q, k, v, gates[0], gates[1], C0[:, 0], n0[:, 0], m0[:, 0])
    h_b, (Cb, nb, mb) = mlstm_chunkwise(jnp.flip(q, 2), jnp.flip(k, 2), jnp.flip(v, 2),
                                        jnp.flip(gates[2], -1), jnp.flip(gates[3], -1),
                                        C0[:, 1], n0[:, 1], m0[:, 1])
    hm = h_f + jnp.flip(h_b, 2)
    hm = hm * lax.rsqrt(jnp.mean(hm * hm, axis=-1, keepdims=True) + EPS) * head_g[:, None, :]
    hm = hm.transpose(0, 2, 1, 3).reshape(B, T, MLSTM_WIDTH) * jax.nn.sigmoid(o)

    u = p[..., OFF_F:]
    if grid_w is None:
        u = u.reshape(B, T, FOURIER_GROUPS, FOURIER_GROUP_DIM)
        fu = jnp.fft.fftn(u, axes=(1, 3), norm="ortho").real
    else:
        rows = T // grid_w
        u = u.reshape(B, rows, grid_w, FOURIER_GROUPS, FOURIER_GROUP_DIM)
        fu = jnp.fft.fftn(u, axes=(1, 2, 4), norm="ortho").real
    fu = fu.reshape(B, T, FOURIER_WIDTH)
    out = jnp.concatenate([hm, fu], axis=-1).astype(h.dtype) @ w_out
    states = (jnp.stack([Cf, Cb], axis=1), jnp.stack([nf, nb], axis=1), jnp.stack([mf, mb], axis=1))
    return out, states


def expert_choice_moe(h, router_w, w1, w3, w2):
    B, T, D = h.shape
    n = B * T
    xf = h.reshape(n, D)
    cap = CAPACITY_FACTOR * n // N_EXPERTS
    aff = jax.nn.softmax((xf @ router_w).astype(jnp.float32), axis=-1)
    gate, idx = lax.top_k(aff.T, cap)
    xe = xf[idx]
    a = jnp.einsum("ecd,edf->ecf", xe, w1)
    bg = jnp.einsum("ecd,edf->ecf", xe, w3)
    ye = jnp.einsum("ecf,efd->ecd", jax.nn.silu(a) * bg, w2) * gate[..., None].astype(h.dtype)
    y = jnp.zeros((n, D), h.dtype).at[idx.reshape(-1)].add(ye.reshape(-1, D))
    return y.reshape(B, T, D)


def modulation(cvec, ada_w, ada_b):
    m = jax.nn.silu(cvec) @ ada_w + ada_b
    return m.reshape(-1, N_MOD, 1, D_MODEL)


def layer(x, mod, init, grid_w, norm1_g, norm2_g, w_in, b_in, head_g, w_out,
          router_w, w1, w3, w2):
    h = rmsnorm(x, norm1_g) * (1 + mod[:, 1]) + mod[:, 0]
    a, states = mixer(h, w_in, b_in, head_g, w_out, init, grid_w)
    x = x + mod[:, 2] * a
    h = rmsnorm(x, norm2_g) * (1 + mod[:, 4]) + mod[:, 3]
    x = x + mod[:, 5] * expert_choice_moe(h, router_w, w1, w3, w2)
    return x, states


def setup_inputs(seed: int = 0) -> dict:
    key = jax.random.key(seed)
    ks = jax.random.split(key, 20)
    D, H, dh, E, F = D_MODEL, MLSTM_HEADS, HEAD_DIM, N_EXPERTS, D_EXPERT
    nrm = jax.random.normal
    b_in = 0.02 * nrm(ks[9], (DEPTH, IN_COLS), jnp.float32)
    fbias = jnp.linspace(3.0, 6.0, H, dtype=jnp.float32)
    b_in = b_in.at[:, OFF_G + H:OFF_G + 2 * H].add(fbias)
    b_in = b_in.at[:, OFF_G + 3 * H:OFF_G + 4 * H].add(fbias)
    return {
        "x_prompt": nrm(ks[0], (BATCH, SEQ, D), jnp.float32),
        "x_sample": nrm(ks[1], (DEC_BATCH, DEC_SEQ, D), jnp.float32),
        "state_C": 0.3 * nrm(ks[2], (DEC_BATCH, DEPTH, 2, H, dh, dh), jnp.float32),
        "state_n": 0.3 * nrm(ks[3], (DEC_BATCH, DEPTH, 2, H, dh), jnp.float32),
        "state_m": nrm(ks[4], (DEC_BATCH, DEPTH, 2, H), jnp.float32),
        "c": nrm(ks[5], (DEC_BATCH, D), jnp.float32),
        "c_ctx": nrm(ks[6], (D,), jnp.float32),
        "ada_w": 0.5 * D ** -0.5 * nrm(ks[7], (DEPTH, D, N_MOD * D), jnp.float32),
        "ada_b": 0.02 * nrm(ks[8], (DEPTH, N_MOD * D), jnp.float32),
        "norm1_g": 1.0 + 0.05 * nrm(ks[10], (DEPTH, D), jnp.float32),
        "norm2_g": 1.0 + 0.05 * nrm(ks[11], (DEPTH, D), jnp.float32),
        "w_in": D ** -0.5 * nrm(ks[12], (DEPTH, D, IN_COLS), jnp.float32),
        "b_in": b_in,
        "head_g": 1.0 + 0.05 * nrm(ks[13], (DEPTH, H, dh), jnp.float32),
        "w_out": MIX_WIDTH ** -0.5 * nrm(ks[14], (DEPTH, MIX_WIDTH, D), jnp.float32),
        "router_w": D ** -0.5 * nrm(ks[15], (DEPTH, D, E), jnp.float32),
        "exp_w1": D ** -0.5 * nrm(ks[16], (DEPTH, E, D, F), jnp.float32),
        "exp_w3": D ** -0.5 * nrm(ks[17], (DEPTH, E, D, F), jnp.float32),
        "exp_w2": F ** -0.5 * nrm(ks[18], (DEPTH, E, F, D), jnp.float32),
        "final_g": 1.0 + 0.05 * nrm(ks[19], (D,), jnp.float32),
    }


def reference(x_prompt, x_sample, state_C, state_n, state_m, c, c_ctx, ada_w, ada_b,
              norm1_g, norm2_g, w_in, b_in, head_g, w_out, router_w, exp_w1, exp_w3,
              exp_w2, final_g):
    xp = x_prompt
    Cs, ns, ms = [], [], []
    for l in range(DEPTH):
        mod_ctx = modulation(c_ctx[None, :], ada_w[l], ada_b[l])
        xp, (Cl, nl, ml) = layer(xp, mod_ctx, None, None, norm1_g[l], norm2_g[l], w_in[l],
                                 b_in[l], head_g[l], w_out[l], router_w[l], exp_w1[l],
                                 exp_w3[l], exp_w2[l])
        Cs.append(Cl)
        ns.append(nl)
        ms.append(ml)
    y_prompt = rmsnorm(xp, final_g)
    new_C = jnp.stack(Cs, axis=1)
    new_n = jnp.stack(ns, axis=1)
    new_m = jnp.stack(ms, axis=1)

    xs = x_sample
    grid_w = GRID_W
    for l in range(DEPTH):
        mod_lat = modulation(c, ada_w[l], ada_b[l])
        init = (state_C[:, l], state_n[:, l], state_m[:, l])
        xs, _ = layer(xs, mod_lat, init, grid_w, norm1_g[l], norm2_g[l], w_in[l], b_in[l],
                      head_g[l], w_out[l], router_w[l], exp_w1[l], exp_w3[l], exp_w2[l])
    y_sample = rmsnorm(xs, final_g)
    return (y_prompt, y_sample, new_C, new_n, new_m)
```

```python
import functools
import math

import numpy as np
import jax
import jax.numpy as jnp
from jax import lax
from jax.experimental import pallas as pl
from jax.experimental.pallas import tpu as pltpu

F32 = jnp.float32
BF16 = jnp.bfloat16
I32 = jnp.int32

D_MODEL = 1024
N_HEADS = 4
HEAD_DIM = 128
MLSTM_WIDTH = N_HEADS * HEAD_DIM
FOURIER_WIDTH = 512
N_GROUPS = 4
GROUP_DIM = 128
CHUNK = 128
GRID_W = 64
N_EXPERTS = 16
CAPACITY_FACTOR = 2
D_EXPERT = 2048
N_MOD = 6
EPS = 1e-6
LANES = 128
AUG = D_MODEL + LANES
VMEM_LIMIT = 56 * 1024 * 1024


def _cparams(sem, vmem=None):
    return pltpu.CompilerParams(dimension_semantics=sem, vmem_limit_bytes=vmem)


def _split3(a):
    a1 = a.astype(BF16)
    r1 = a - a1.astype(F32)
    a2 = r1.astype(BF16)
    a3 = (r1 - a2.astype(F32)).astype(BF16)
    return a1, a2, a3


def _dot(a, b):
    return jnp.dot(a, b, preferred_element_type=F32)


def _dot3(a, b):
    a1, a2, _ = _split3(a)
    b1, b2, _ = _split3(b)
    return _dot(a1, b1) + (_dot(a1, b2) + _dot(a2, b1))


def _mod_kernel(c_ref, w_ref, b_ref, o_ref):
    c = c_ref[...]
    s = c * jax.nn.sigmoid(c)
    o_ref[...] = _dot(s.astype(BF16), w_ref[...].astype(BF16)) + b_ref[...]


def _modulation(cvec8, ada_w, ada_b):
    tn = D_MODEL
    out = pl.pallas_call(
        _mod_kernel,
        grid=(N_MOD,),
        in_specs=[pl.BlockSpec((8, D_MODEL), lambda j: (0, 0)),
                  pl.BlockSpec((D_MODEL, tn), lambda j: (0, j)),
                  pl.BlockSpec((1, tn), lambda j: (0, j))],
        out_specs=pl.BlockSpec((8, tn), lambda j: (0, j)),
        out_shape=jax.ShapeDtypeStruct((8, N_MOD * D_MODEL), F32),
        compiler_params=_cparams(("arbitrary",)),
        name="modulation",
    )(cvec8, ada_w, ada_b.reshape(1, -1))
    return out.reshape(8, N_MOD, D_MODEL)


def _mod_index(row0, tiles_per_batch):
    return lambda i: (row0 + i // tiles_per_batch, 0, 0)


def _in_kernel(x_ref, mod_ref, g_ref, wq_ref, bq_ref, sc_ref, wg_ref, bg_ref, wf_ref, bf_ref,
               qkvo_ref, gates_ref, u_ref):
    x = x_ref[...]
    ms = jnp.mean(x * x, axis=-1, keepdims=True)
    y = x * lax.rsqrt(ms + EPS) * g_ref[...]
    h = y * (1.0 + mod_ref[0, 1:2, :]) + mod_ref[0, 0:1, :]
    hb = h.astype(BF16)
    p = (_dot(hb, wq_ref[...]) + bq_ref[...]) * sc_ref[...]
    qkvo_ref[...] = p.astype(BF16)
    gates_ref[...] = _dot(hb, wg_ref[...]) + bg_ref[...]
    u_ref[...] = _dot(hb, wf_ref[...]) + bf_ref[...]


def _in_proj(x2, mod, row0, tiles_per_batch, tm, g1, wq, bq, sc, wg, bg, wf, bfo):
    n = x2.shape[0]
    wq_n = wq.shape[1]
    const = lambda i: (0, 0)
    return pl.pallas_call(
        _in_kernel,
        grid=(n // tm,),
        in_specs=[pl.BlockSpec((tm, D_MODEL), lambda i: (i, 0)),
                  pl.BlockSpec((1, N_MOD, D_MODEL), _mod_index(row0, tiles_per_batch)),
                  pl.BlockSpec((1, D_MODEL), const),
                  pl.BlockSpec((D_MODEL, wq_n), const), pl.BlockSpec((1, wq_n), const),
                  pl.BlockSpec((1, wq_n), const),
                  pl.BlockSpec((D_MODEL, LANES), const), pl.BlockSpec((1, LANES), const),
                  pl.BlockSpec((D_MODEL, FOURIER_WIDTH), const), pl.BlockSpec((1, FOURIER_WIDTH), const)],
        out_specs=[pl.BlockSpec((tm, wq_n), lambda i: (i, 0)),
                   pl.BlockSpec((tm, LANES), lambda i: (i, 0)),
                   pl.BlockSpec((tm, FOURIER_WIDTH), lambda i: (i, 0))],
        out_shape=[jax.ShapeDtypeStruct((n, wq_n), BF16),
                   jax.ShapeDtypeStruct((n, LANES), F32),
                   jax.ShapeDtypeStruct((n, FOURIER_WIDTH), F32)],
        compiler_params=_cparams(("arbitrary",), VMEM_LIMIT),
        name="in_proj",
    )(x2, mod, g1, wq, bq, sc, wg, bg, wf, bfo)


def _gates_kernel(g_ref, bc_ref, bct_ref):
    g = g_ref[...]
    ls = jnp.minimum(g, 0.0) - jnp.log(1.0 + jnp.exp(-jnp.abs(g)))
    row = lax.broadcasted_iota(I32, (CHUNK, CHUNK), 0)
    col = lax.broadcasted_iota(I32, (CHUNK, CHUNK), 1)
    tril = (col <= row).astype(BF16)
    triu = (col >= row).astype(BF16)
    l1, l2, l3 = _split3(ls)
    pre = _dot(tril, l1) + _dot(tril, l2) + _dot(tril, l3)
    suf = _dot(triu, l1) + _dot(triu, l2) + _dot(triu, l3)
    lane = lax.broadcasted_iota(I32, (CHUNK, LANES), 1)
    bc = jnp.where((lane >= 4) & (lane < 8), pre,
                   jnp.where((lane >= 12) & (lane < 16), suf, g))
    bc_ref[...] = bc
    bct_ref[0] = bc.T


def _gate_sums(gates):
    n = gates.shape[0]
    nc = n // CHUNK
    return pl.pallas_call(
        _gates_kernel,
        grid=(nc,),
        in_specs=[pl.BlockSpec((CHUNK, LANES), lambda i: (i, 0))],
        out_specs=[pl.BlockSpec((CHUNK, LANES), lambda i: (i, 0)),
                   pl.BlockSpec((1, LANES, CHUNK), lambda i: (i, 0, 0))],
        out_shape=[jax.ShapeDtypeStruct((n, LANES), F32),
                   jax.ShapeDtypeStruct((nc, LANES, CHUNK), F32)],
        compiler_params=_cparams(("arbitrary",)),
        name="gate_sums",
    )(gates)


def _mlstm_kernel(q_ref, k_ref, v_ref, o_ref, bc_ref, bct_ref, c0_ref, n0_ref, m0_ref, hg_ref,
                  hm_ref, cf_ref, nf_ref, mf_ref, hf_s, hb_s, c_s, n_s, m_s, *, n_chunks):
    head = pl.program_id(1)
    c_s[...] = c0_ref[0, :, 0]
    n_s[...] = n0_ref[0, :, 0]
    m_s[...] = m0_ref[0, :, 0]
    row = lax.broadcasted_iota(I32, (CHUNK, CHUNK), 0)
    col = lax.broadcasted_iota(I32, (CHUNK, CHUNK), 1)
    lane = lax.broadcasted_iota(I32, (CHUNK, LANES), 1)
    sub = lax.broadcasted_iota(I32, (LANES, CHUNK), 0)
    nt = (((1,), (1,)), ((), ()))
    tn = (((0,), (0,)), ((), ()))

    def one_direction(c, d, h_out):
        sl = pl.ds(pl.multiple_of(c * CHUNK, CHUNK), CHUNK)
        q = q_ref[0, sl, :]
        k = k_ref[0, sl, :]
        v = v_ref[0, sl, :]
        bc = bc_ref[0, sl, :]
        bct = bct_ref[0, c]
        ich = 8 * d + head
        fch = 8 * d + 4 + head
        ig_col = jnp.sum(jnp.where(lane == ich, bc, 0.0), axis=1, keepdims=True)
        b_col = jnp.sum(jnp.where(lane == fch, bc, 0.0), axis=1, keepdims=True)
        ig_row = jnp.sum(jnp.where(sub == ich, bct, 0.0), axis=0, keepdims=True)
        b_row = jnp.sum(jnp.where(sub == fch, bct, 0.0), axis=0, keepdims=True)
        if d == 0:
            g = b_col[CHUNK - 1:CHUNK, :]
            mask = col <= row
        else:
            g = b_col[0:1, :]
            mask = col >= row
        c_prev = c_s[d]
        n_prev = n_s[d]
        m_prev = m_s[d][:, 0:1]
        w_log = g - b_col + ig_col
        m_loc = jnp.max(w_log, axis=0, keepdims=True)
        w = jnp.exp(w_log - m_loc)
        wk = w * k.astype(F32)
        c_loc = lax.dot_general(v, wk.astype(BF16), tn, preferred_element_type=F32)
        n_loc = jnp.sum(wk, axis=0, keepdims=True)
        dmat = jnp.where(mask, b_col - b_row + ig_row, -jnp.inf)
        inter_log = b_col + m_prev
        m_comb = jnp.maximum(inter_log, jnp.max(dmat, axis=1, keepdims=True))
        s = lax.dot_general(q, k, nt, preferred_element_type=F32) * jnp.exp(dmat - m_comb)
        w_inter = jnp.exp(inter_log - m_comb)
        cq = lax.dot_general(q, c_prev.astype(BF16), nt, preferred_element_type=F32)
        num = _dot(s.astype(BF16), v) + w_inter * cq
        qn = jnp.sum(q.astype(F32) * n_prev, axis=1, keepdims=True)
        den = jnp.sum(s, axis=1, keepdims=True) + w_inter * qn
        h_out[sl, :] = num / jnp.maximum(jnp.abs(den), jnp.exp(-m_comb))
        m_new = jnp.maximum(g + m_prev, m_loc)
        a = jnp.exp(g + m_prev - m_new)
        bb = jnp.exp(m_loc - m_new)
        c_s[d] = a * c_prev + bb * c_loc
        n_s[d] = a * n_prev + bb * n_loc
        m_s[d] = jnp.broadcast_to(m_new, (1, LANES))

    def scan_body(i, carry):
        one_direction(i, 0, hf_s)
        one_direction(n_chunks - 1 - i, 1, hb_s)
        return carry

    lax.fori_loop(0, n_chunks, scan_body, 0)

    def out_body(c, carry):
        sl = pl.ds(pl.multiple_of(c * CHUNK, CHUNK), CHUNK)
        hs = hf_s[sl, :] + hb_s[sl, :]
        hn = hs * lax.rsqrt(jnp.mean(hs * hs, axis=-1, keepdims=True) + EPS) * hg_ref[0]
        hm_ref[0, sl, :] = (hn * jax.nn.sigmoid(o_ref[0, sl, :].astype(F32))).astype(BF16)
        return carry

    lax.fori_loop(0, n_chunks, out_body, 0)
    cf_ref[0, :, 0] = c_s[...]
    nf_ref[0, :, 0] = n_s[...]
    mf_ref[0, :, 0] = m_s[...]


def _mlstm(qkvo3, bc3, bct4, c0, n0, m0, head_g3):
    b, t, _ = qkvo3.shape
    nc = t // CHUNK
    blk = lambda off: pl.BlockSpec((1, t, HEAD_DIM), lambda i, h: (i, 0, off + h))
    st_c = pl.BlockSpec((1, 2, 1, HEAD_DIM, HEAD_DIM), lambda i, h: (i, 0, h, 0, 0))
    st_v = pl.BlockSpec((1, 2, 1, 1, LANES), lambda i, h: (i, 0, h, 0, 0))
    return pl.pallas_call(
        functools.partial(_mlstm_kernel, n_chunks=nc),
        grid=(b, N_HEADS),
        in_specs=[blk(0), blk(N_HEADS), blk(2 * N_HEADS), blk(3 * N_HEADS),
                  pl.BlockSpec((1, t, LANES), lambda i, h: (i, 0, 0)),
                  pl.BlockSpec((1, nc, LANES, CHUNK), lambda i, h: (i, 0, 0, 0)),
                  st_c, st_v, st_v,
                  pl.BlockSpec((1, 1, HEAD_DIM), lambda i, h: (h, 0, 0))],
        out_specs=[pl.BlockSpec((1, t, HEAD_DIM), lambda i, h: (i, 0, h)), st_c, st_v, st_v],
        out_shape=[jax.ShapeDtypeStruct((b, t, MLSTM_WIDTH), BF16),
                   jax.ShapeDtypeStruct((b, 2, N_HEADS, HEAD_DIM, HEAD_DIM), F32),
                   jax.ShapeDtypeStruct((b, 2, N_HEADS, 1, LANES), F32),
                   jax.ShapeDtypeStruct((b, 2, N_HEADS, 1, LANES), F32)],
        scratch_shapes=[pltpu.VMEM((t, HEAD_DIM), F32), pltpu.VMEM((t, HEAD_DIM), F32),
                        pltpu.VMEM((2, HEAD_DIM, HEAD_DIM), F32), pltpu.VMEM((2, 1, LANES), F32),
                        pltpu.VMEM((2, 1, LANES), F32)],
        compiler_params=_cparams(("arbitrary", "arbitrary"), VMEM_LIMIT),
        name="mlstm",
    )(qkvo3, qkvo3, qkvo3, qkvo3, bc3, bct4, c0, n0, m0, head_g3)


def _dft_cos_sin(n):
    k = np.arange(n)
    ang = 2.0 * np.pi * ((k[:, None] * k[None, :]) % n) / n
    return np.cos(ang), np.sin(ang)


def _channel_dft(scale):
    cd, sd = _dft_cos_sin(GROUP_DIM)
    eye = np.eye(N_GROUPS)
    return (jnp.asarray(np.kron(eye, cd) * scale, F32), jnp.asarray(np.kron(eye, -sd) * scale, F32))


def _fourier_ctx_kernel(u_ref, bdc_ref, bds_ref, ct_ref, st_ref, o_ref):
    u = u_ref[0]
    a = _dot3(u, bdc_ref[...])
    b = _dot3(u, bds_ref[...])
    o_ref[0] = (_dot3(ct_ref[...], a) + _dot3(st_ref[...], b)).astype(BF16)


def _fourier_ctx(u3):
    b, t, w = u3.shape
    bdc, bds = _channel_dft(1.0 / math.sqrt(t * GROUP_DIM))
    ct, st = _dft_cos_sin(t)
    const = lambda i: (0, 0)
    return pl.pallas_call(
        _fourier_ctx_kernel,
        grid=(b,),
        in_specs=[pl.BlockSpec((1, t, w), lambda i: (i, 0, 0)),
                  pl.BlockSpec((w, w), const), pl.BlockSpec((w, w), const),
                  pl.BlockSpec((t, t), const), pl.BlockSpec((t, t), const)],
        out_specs=pl.BlockSpec((1, t, w), lambda i: (i, 0, 0)),
        out_shape=jax.ShapeDtypeStruct((b, t, w), BF16),
        compiler_params=_cparams(("arbitrary",)),
        name="fourier_ctx",
    )(u3, bdc, bds, jnp.asarray(ct, F32), jnp.asarray(st, F32))


def _fourier_chan_kernel(u_ref, bdc_ref, bds_ref, a_ref, b_ref):
    u = u_ref[0]
    a_ref[0] = _dot3(u, bdc_ref[...]).astype(BF16)
    b_ref[0] = _dot3(u, bds_ref[...]).astype(BF16)


def _fourier_pos_kernel(cp_ref, sp_ref, a_ref, b_ref, o_ref):
    o_ref[0] = (_dot(cp_ref[...], a_ref[0]) + _dot(sp_ref[...], b_ref[0])).astype(BF16)


def _fourier_lat(u3, grid_w):
    b, t, w = u3.shape
    rows = t // grid_w
    bdc, bds = _channel_dft(1.0 / math.sqrt(t * GROUP_DIM))
    tm = 512
    const = lambda i, j: (0, 0)
    a, bm = pl.pallas_call(
        _fourier_chan_kernel,
        grid=(b, t // tm),
        in_specs=[pl.BlockSpec((1, tm, w), lambda i, j: (i, j, 0)),
                  pl.BlockSpec((w, w), const), pl.BlockSpec((w, w), const)],
        out_specs=[pl.BlockSpec((1, tm, w), lambda i, j: (i, j, 0))] * 2,
        out_shape=[jax.ShapeDtypeStruct((b, t, w), BF16)] * 2,
        compiler_params=_cparams(("arbitrary", "arbitrary")),
        name="fourier_chan",
    )(u3, bdc, bds)
    cr, sr = _dft_cos_sin(rows)
    cc, sc = _dft_cos_sin(grid_w)
    cr, sr, cc, sc = (jnp.asarray(m, F32) for m in (cr, sr, cc, sc))
    kron = lambda x, y: (x[:, None, :, None] * y[None, :, None, :]).reshape(t, t)
    cpos = (kron(cr, cc) - kron(sr, sc)).astype(BF16)
    spos = (kron(sr, cc) + kron(cr, sc)).astype(BF16)
    return pl.pallas_call(
        _fourier_pos_kernel,
        grid=(t // tm, b),
        in_specs=[pl.BlockSpec((tm, t), lambda j, i: (j, 0)),
                  pl.BlockSpec((tm, t), lambda j, i: (j, 0)),
                  pl.BlockSpec((1, t, w), lambda j, i: (i, 0, 0)),
                  pl.BlockSpec((1, t, w), lambda j, i: (i, 0, 0))],
        out_specs=pl.BlockSpec((1, tm, w), lambda j, i: (i, j, 0)),
        out_shape=jax.ShapeDtypeStruct((b, t, w), BF16),
        compiler_params=_cparams(("arbitrary", "arbitrary"), VMEM_LIMIT),
        name="fourier_pos",
    )(cpos, spos, a, bm)


def _out_kernel(hm_ref, fu_ref, x_ref, mod_ref, g_ref, wa_ref, wb_ref, rw_ref,
                x1_ref, aug_ref, afft_ref):
    a = _dot(hm_ref[...], wa_ref[...]) + _dot(fu_ref[...], wb_ref[...])
    x1 = x_ref[...] + mod_ref[0, 2:3, :] * a
    x1_ref[...] = x1
    ms = jnp.mean(x1 * x1, axis=-1, keepdims=True)
    y = x1 * lax.rsqrt(ms + EPS) * g_ref[...]
    h2 = y * (1.0 + mod_ref[0, 4:5, :]) + mod_ref[0, 3:4, :]
    logits = _dot3(h2, rw_ref[...])
    lane = lax.broadcasted_iota(I32, logits.shape, 1)
    valid = lane < N_EXPERTS
    lg = jnp.where(valid, logits, -1e30)
    ex = jnp.where(valid, jnp.exp(lg - jnp.max(lg, axis=1, keepdims=True)), 0.0)
    aff = ex / jnp.sum(ex, axis=1, keepdims=True)
    aug_ref[:, :D_MODEL] = h2
    aug_ref[:, D_MODEL:] = aff
    afft_ref[...] = aff.T[:N_EXPERTS, :]


def _out_proj(hm2, fu2, x2, mod, row0, tiles_per_batch, tm, g2, wa, wb, rw):
    n = x2.shape[0]
    const = lambda i: (0, 0)
    return pl.pallas_call(
        _out_kernel,
        grid=(n // tm,),
        in_specs=[pl.BlockSpec((tm, MLSTM_WIDTH), lambda i: (i, 0)),
                  pl.BlockSpec((tm, FOURIER_WIDTH), lambda i: (i, 0)),
                  pl.BlockSpec((tm, D_MODEL), lambda i: (i, 0)),
                  pl.BlockSpec((1, N_MOD, D_MODEL), _mod_index(row0, tiles_per_batch)),
                  pl.BlockSpec((1, D_MODEL), const),
                  pl.BlockSpec((MLSTM_WIDTH, D_MODEL), const),
                  pl.BlockSpec((FOURIER_WIDTH, D_MODEL), const),
                  pl.BlockSpec((D_MODEL, LANES), const)],
        out_specs=[pl.BlockSpec((tm, D_MODEL), lambda i: (i, 0)),
                   pl.BlockSpec((tm, AUG), lambda i: (i, 0)),
                   pl.BlockSpec((N_EXPERTS, tm), lambda i: (0, i))],
        out_shape=[jax.ShapeDtypeStruct((n, D_MODEL), F32),
                   jax.ShapeDtypeStruct((n, AUG), F32),
                   jax.ShapeDtypeStruct((N_EXPERTS, n), F32)],
        compiler_params=_cparams(("arbitrary",), VMEM_LIMIT),
        name="out_proj",
    )(hm2, fu2, x2, mod, g2, wa, wb, rw)


def _topk_kernel(aff_ref, pos_ref, blk_ref, *, n, cap):
    nb = n // LANES
    aff = aff_ref[...]

    def enough(t):
        return jnp.sum(jnp.where(aff >= t, 1.0, 0.0), axis=1, keepdims=True) >= cap

    def pow2(k):
        return lax.bitcast_convert_type(jnp.left_shift(127 - k, 23), F32)

    def exp_search(_, c):
        lo, hi = c
        mid = jnp.right_shift(lo + hi, 1)
        ok = enough(pow2(jnp.minimum(mid, 126)))
        return jnp.where(ok, lo, mid + 1), jnp.where(ok, mid, hi)

    kz = jnp.zeros((N_EXPERTS, 1), I32)
    kstar, _ = lax.fori_loop(0, 7, exp_search, (kz, kz + 127))
    found = kstar < 127
    p = pow2(jnp.minimum(kstar, 126))
    t_lo0 = jnp.where(found, p, 0.0)
    t_hi0 = jnp.where(found, 2.0 * p, p)

    def bisect(_, c):
        t_lo, t_hi = c
        mid = 0.5 * (t_lo + t_hi)
        ok = enough(mid)
        return jnp.where(ok, mid, t_lo), jnp.where(ok, t_hi, mid)

    t_lo, t_hi = lax.fori_loop(0, 40, bisect, (t_lo0, t_hi0))
    n_gt = jnp.sum(jnp.where(aff >= t_hi, 1.0, 0.0), axis=1, keepdims=True)
    need = cap - n_gt
    row = lax.broadcasted_iota(I32, (LANES, LANES), 0)
    col = lax.broadcasted_iota(I32, (LANES, LANES), 1)
    upper = (row <= col).astype(BF16)
    blane = lax.broadcasted_iota(I32, blk_ref.shape, 1)

    def block(b, carry):
        eq_off, sel_off = carry
        sl = pl.ds(pl.multiple_of(b * LANES, LANES), LANES)
        ab = aff_ref[:, sl]
        gt = ab >= t_hi
        eq = (ab >= t_lo) & (ab < t_hi)
        eqf = jnp.where(eq, 1.0, 0.0)
        eq_rank = _dot(eqf.astype(BF16), upper) + eq_off - eqf
        sel = gt | (eq & (eq_rank < need))
        self_ = jnp.where(sel, 1.0, 0.0)
        cum = _dot(self_.astype(BF16), upper) + sel_off
        pos_ref[:, sl] = jnp.where(sel, cum - 1.0, -1.0).astype(I32)
        blk_ref[...] = jnp.where(blane == b, sel_off.astype(I32), blk_ref[...])
        return (eq_off + jnp.sum(eqf, axis=1, keepdims=True),
                sel_off + jnp.sum(self_, axis=1, keepdims=True))

    blk_ref[...] = jnp.full(blk_ref.shape, cap, I32)
    zero = jnp.zeros((N_EXPERTS, 1), F32)
    lax.fori_loop(0, nb, block, (zero, zero))


def _topk(aff_t, cap):
    n = aff_t.shape[1]
    return pl.pallas_call(
        functools.partial(_topk_kernel, n=n, cap=cap),
        grid=(1,),
        in_specs=[pl.BlockSpec((N_EXPERTS, n), lambda i: (0, 0))],
        out_specs=[pl.BlockSpec((N_EXPERTS, n), lambda i: (0, 0)),
                   pl.BlockSpec((N_EXPERTS, 2 * LANES), lambda i: (0, 0))],
        out_shape=[jax.ShapeDtypeStruct((N_EXPERTS, n), I32),
                   jax.ShapeDtypeStruct((N_EXPERTS, 2 * LANES), I32)],
        compiler_params=_cparams(("arbitrary",)),
        name="topk",
    )(aff_t)


def _compact_kernel(blk_sm, pos_ref, out_ref, *, nb):
    e = pl.program_id(0)
    j0 = pl.program_id(1) * LANES

    def count_le(x):
        lo, hi = jnp.int32(0), jnp.int32(nb + 1)
        for _ in range(9):
            mid = (lo + hi) >> 1
            le = (blk_sm[e, mid] <= x) & (lo < hi)
            lo, hi = jnp.where(le, mid + 1, lo), jnp.where(le | (lo >= hi), hi, mid)
        return lo

    first = count_le(j0) - 1
    last = jnp.minimum(count_le(j0 + LANES - 1), nb)
    slot = lax.broadcasted_iota(I32, (LANES, LANES), 0) + j0
    tok = lax.broadcasted_iota(I32, (LANES, LANES), 1)

    def body(b, acc):
        prow = pos_ref[0, :, pl.ds(pl.multiple_of(b * LANES, LANES), LANES)]
        return acc + jnp.where(prow == slot, (tok + b * LANES).astype(F32), 0.0)

    acc = lax.fori_loop(first, last, body, jnp.zeros((LANES, LANES), F32))
    out_ref[0, 0] = jnp.sum(acc.T, axis=0, keepdims=True).astype(I32)


def _compact(pos, blk, cap):
    n = pos.shape[1]
    ncj = cap // LANES
    out = pl.pallas_call(
        functools.partial(_compact_kernel, nb=n // LANES),
        grid_spec=pltpu.PrefetchScalarGridSpec(
            num_scalar_prefetch=1,
            grid=(N_EXPERTS, ncj),
            in_specs=[pl.BlockSpec((1, 1, n), lambda e, j, blk: (e, 0, 0))],
            out_specs=pl.BlockSpec((1, 1, 1, LANES), lambda e, j, blk: (e, j, 0, 0)),
        ),
        out_shape=jax.ShapeDtypeStruct((N_EXPERTS, ncj, 1, LANES), I32),
        compiler_params=_cparams(("arbitrary", "arbitrary")),
        name="compact",
    )(blk, pos.reshape(N_EXPERTS, 1, n))
    return out.reshape(N_EXPERTS, cap)


def _moe_kernel(lists_sm, aug_c, aug_l, yc_in, yl_in, w1_ref, w3_ref, w2_ref, yc_out, yl_out,
                xbuf, xe, gate, acc, w1b, w3b, w2b, sem, *, cap_c, cap_l, row_chunk):
    del yc_in, yl_in
    e = pl.program_id(0)
    f = pl.program_id(1)
    nf = pl.num_programs(1)
    rows = cap_c + cap_l
    base = e * rows

    def row_copies(start_fn):
        def ctx_body(j, carry):
            start_fn(aug_c, yc_out, lists_sm[base + j], j)
            return carry

        def lat_body(j, carry):
            start_fn(aug_l, yl_out, lists_sm[base + j], j)
            return carry

        lax.fori_loop(0, cap_c, ctx_body, 0)
        lax.fori_loop(cap_c, rows, lat_body, 0)

    def wait_rows(width):
        pltpu.make_async_copy(xbuf.at[:, pl.ds(0, width)], xbuf.at[:, pl.ds(0, width)], sem.at[0]).wait()

    @pl.when(f == 0)
    def _():
        def start(aug, y, t, j):
            pltpu.make_async_copy(aug.at[pl.ds(t, 1)], xbuf.at[pl.ds(j, 1)], sem.at[0]).start()

        row_copies(start)
        wait_rows(AUG)
        xe[...] = xbuf[:, :D_MODEL].astype(BF16)
        lane = lax.broadcasted_iota(I32, (rows, LANES), 1)
        gate[...] = jnp.sum(jnp.where(lane == e, xbuf[:, D_MODEL:], 0.0), axis=1, keepdims=True)

    w1b[...] = w1_ref[0].astype(BF16)
    w3b[...] = w3_ref[0].astype(BF16)
    w2b[...] = w2_ref[0].astype(BF16)
    for r in range(rows // row_chunk):
        sl = pl.ds(r * row_chunk, row_chunk)
        x = xe[sl, :]
        a = _dot(x, w1b[...])
        bg = _dot(x, w3b[...])
        act = (a * jax.nn.sigmoid(a) * bg).astype(BF16)
        part = _dot(act, w2b[...])

        @pl.when(f == 0)
        def _():
            acc[sl, :] = part

        @pl.when(f > 0)
        def _():
            acc[sl, :] += part

    @pl.when(f == nf - 1)
    def _():
        def start_in(aug, y, t, j):
            pltpu.make_async_copy(y.at[pl.ds(t, 1)], xbuf.at[pl.ds(j, 1), pl.ds(0, D_MODEL)], sem.at[0]).start()

        row_copies(start_in)
        wait_rows(D_MODEL)
        xbuf[:, :D_MODEL] = xbuf[:, :D_MODEL] + acc[...] * gate[...]

        def start_out(aug, y, t, j):
            pltpu.make_async_copy(xbuf.at[pl.ds(j, 1), pl.ds(0, D_MODEL)], y.at[pl.ds(t, 1)], sem.at[0]).start()

        row_copies(start_out)
        wait_rows(D_MODEL)


def _moe(lists, aug_c, aug_l, w1, w3, w2, cap_c, cap_l, tf=256, row_chunk=512):
    rows = cap_c + cap_l
    n_c, n_l = aug_c.shape[0], aug_l.shape[0]
    yc0 = jnp.zeros((n_c, D_MODEL), F32)
    yl0 = jnp.zeros((n_l, D_MODEL), F32)
    any_spec = pl.BlockSpec(memory_space=pl.ANY)
    return pl.pallas_call(
        functools.partial(_moe_kernel, cap_c=cap_c, cap_l=cap_l, row_chunk=row_chunk),
        grid_spec=pltpu.PrefetchScalarGridSpec(
            num_scalar_prefetch=1,
            grid=(N_EXPERTS, D_EXPERT // tf),
            in_specs=[any_spec, any_spec, any_spec, any_spec,
                      pl.BlockSpec((1, D_MODEL, tf), lambda e, f, ls: (e, 0, f)),
                      pl.BlockSpec((1, D_MODEL, tf), lambda e, f, ls: (e, 0, f)),
                      pl.BlockSpec((1, tf, D_MODEL), lambda e, f, ls: (e, f, 0))],
            out_specs=[any_spec, any_spec],
            scratch_shapes=[pltpu.VMEM((rows, AUG), F32), pltpu.VMEM((rows, D_MODEL), BF16),
                            pltpu.VMEM((rows, 1), F32), pltpu.VMEM((rows, D_MODEL), F32),
                            pltpu.VMEM((D_MODEL, tf), BF16), pltpu.VMEM((D_MODEL, tf), BF16),
                            pltpu.VMEM((tf, D_MODEL), BF16), pltpu.SemaphoreType.DMA((1,))],
        ),
        out_shape=[jax.ShapeDtypeStruct((n_c, D_MODEL), F32), jax.ShapeDtypeStruct((n_l, D_MODEL), F32)],
        input_output_aliases={3: 0, 4: 1},
        compiler_params=_cparams(("arbitrary", "arbitrary"), VMEM_LIMIT),
        name="moe",
    )(lists, aug_c, aug_l, yc0, yl0, w1, w3, w2)


def _final_kernel(x1_ref, y_ref, mod_ref, g_ref, o_ref):
    x = x1_ref[...] + mod_ref[0, 5:6, :] * y_ref[...]
    ms = jnp.mean(x * x, axis=-1, keepdims=True)
    o_ref[...] = x * lax.rsqrt(ms + EPS) * g_ref[...]


def _final(x1, y, mod, row0, tiles_per_batch, tm, gf):
    n = x1.shape[0]
    tile = pl.BlockSpec((tm, D_MODEL), lambda i: (i, 0))
    return pl.pallas_call(
        _final_kernel,
        grid=(n // tm,),
        in_specs=[tile, tile,
                  pl.BlockSpec((1, N_MOD, D_MODEL), _mod_index(row0, tiles_per_batch)),
                  pl.BlockSpec((1, D_MODEL), lambda i: (0, 0))],
        out_specs=tile,
        out_shape=jax.ShapeDtypeStruct((n, D_MODEL), F32),
        compiler_params=_cparams(("arbitrary",)),
        name="final_norm",
    )(x1, y, mod, gf)


def kernel(x_prompt, x_sample, state_C, state_n, state_m, c, c_ctx, ada_w, ada_b, norm1_g, norm2_g,
           w_in, b_in, head_g, w_out, router_w, exp_w1, exp_w3, exp_w2, final_g):
    depth = ada_w.shape[0]
    assert depth == 1, "single-layer trunk"
    bc_, tc_, _ = x_prompt.shape
    bl_, tl_, _ = x_sample.shape
    w4 = 4 * MLSTM_WIDTH
    tm = 512

    cvec = jnp.concatenate([c_ctx[None, :], c, jnp.zeros((8 - 1 - bl_, D_MODEL), F32)], axis=0)
    mod = _modulation(cvec, ada_w[0], ada_b[0])

    wi = w_in[0]
    bi = b_in[0]
    wq = wi[:, :w4].astype(BF16)
    bq = bi[None, :w4]
    wg = jnp.pad(wi[:, w4:w4 + 16], ((0, 0), (0, LANES - 16))).astype(BF16)
    bg = jnp.pad(bi[w4:w4 + 16], (0, LANES - 16))[None, :]
    wf = wi[:, w4 + 16:].astype(BF16)
    bfo = bi[None, w4 + 16:]
    colscale = jnp.ones((1, w4), F32).at[:, MLSTM_WIDTH:2 * MLSTM_WIDTH].set(HEAD_DIM ** -0.5)
    g1 = norm1_g[0][None, :]
    g2 = norm2_g[0][None, :]
    wo = w_out[0].astype(BF16)
    wa, wb = wo[:MLSTM_WIDTH], wo[MLSTM_WIDTH:]
    rw = jnp.pad(router_w[0], ((0, 0), (0, LANES - N_EXPERTS)))
    hg3 = head_g[0][:, None, :]

    def spread(v):
        return jnp.broadcast_to(v[..., None, None], v.shape + (1, LANES))

    def mixer_and_router(x3, row0, init, grid_w):
        b, t, _ = x3.shape
        x2 = x3.reshape(b * t, D_MODEL)
        tpb = max(t // tm, 1) if row0 else (b * t) // tm + 1
        qkvo, gates, u = _in_proj(x2, mod, row0, tpb, tm, g1, wq, bq, colscale, wg, bg, wf, bfo)
        bcs, bct = _gate_sums(gates)
        c0, n0, m0 = init
        hm, cf, nf, mf = _mlstm(qkvo.reshape(b, t, w4), bcs.reshape(b, t, LANES),
                                bct.reshape(b, t // CHUNK, LANES, CHUNK), c0, n0, m0, hg3)
        u3 = u.reshape(b, t, FOURIER_WIDTH)
        fu = _fourier_ctx(u3) if grid_w is None else _fourier_lat(u3, grid_w)
        x1, aug, aff_t = _out_proj(hm.reshape(b * t, MLSTM_WIDTH), fu.reshape(b * t, FOURIER_WIDTH),
                                   x2, mod, row0, tpb, tm, g2, wa, wb, rw)
        cap = CAPACITY_FACTOR * (b * t) // N_EXPERTS
        pos, blk = _topk(aff_t, cap)
        return x1, aug, _compact(pos, blk, cap), cap, (cf, nf, mf), tpb

    zero_init = (jnp.zeros((bc_, 2, N_HEADS, HEAD_DIM, HEAD_DIM), F32),
                 jnp.zeros((bc_, 2, N_HEADS, 1, LANES), F32),
                 jnp.zeros((bc_, 2, N_HEADS, 1, LANES), F32))
    lat_init = (state_C[:, 0], state_n[:, 0][..., None, :], spread(state_m[:, 0]))
    x1c, augc, listc, capc, (cf, nf, mf), tpbc = mixer_and_router(x_prompt, 0, zero_init, None)
    x1l, augl, listl, capl, _, tpbl = mixer_and_router(x_sample, 1, lat_init, GRID_W)

    lists = jnp.concatenate([listc, listl], axis=1).reshape(-1)
    yc, yl = _moe(lists, augc, augl, exp_w1[0], exp_w3[0], exp_w2[0], capc, capl)
    gf = final_g[None, :]
    y_prompt = _final(x1c, yc, mod, 0, tpbc, tm, gf).reshape(x_prompt.shape)
    y_sample = _final(x1l, yl, mod, 1, tpbl, tm, gf).reshape(x_sample.shape)
    new_c = cf[:, None]
    new_n = nf[:, None, :, :, 0, :]
    new_m = mf[:, None, :, :, 0, 0]
    return (y_prompt, y_sample, new_c, new_n, new_m)
```

```python
import functools
import math

import numpy as np
import jax
import jax.numpy as jnp
from jax import lax
from jax.experimental import pallas as pl
from jax.experimental.pallas import tpu as pltpu

F32 = jnp.float32
BF16 = jnp.bfloat16
I32 = jnp.int32

D_MODEL = 1024
N_HEADS = 4
HEAD_DIM = 128
MLSTM_WIDTH = N_HEADS * HEAD_DIM
FOURIER_WIDTH = 512
N_GROUPS = 4
GROUP_DIM = 128
CHUNK = 128
GRID_W = 64
N_EXPERTS = 16
CAPACITY_FACTOR = 2
D_EXPERT = 2048
N_MOD = 6
EPS = 1e-6
LANES = 128
SUBLANES = 8
AUG = D_MODEL + LANES
WINDOW_SHIFT = 6
WINDOW = 1 << WINDOW_SHIFT
VMEM_LIMIT = 56 * 1024 * 1024


def _cparams(sem, vmem=None):
    return pltpu.CompilerParams(dimension_semantics=sem, vmem_limit_bytes=vmem)


def _split3(a):
    a1 = a.astype(BF16)
    r1 = a - a1.astype(F32)
    a2 = r1.astype(BF16)
    a3 = (r1 - a2.astype(F32)).astype(BF16)
    return a1, a2, a3


def _dot(a, b):
    return jnp.dot(a, b, preferred_element_type=F32)


def _dot3(a, b):
    a1, a2, _ = _split3(a)
    b1, b2, _ = _split3(b)
    return _dot(a1, b1) + (_dot(a1, b2) + _dot(a2, b1))


def _mod_kernel(c_ref, w_ref, b_ref, o_ref):
    c = c_ref[...]
    s = c * jax.nn.sigmoid(c)
    o_ref[...] = _dot(s.astype(BF16), w_ref[...].astype(BF16)) + b_ref[...]


def _modulation(cvec8, ada_w, ada_b):
    tn = D_MODEL
    out = pl.pallas_call(
        _mod_kernel,
        grid=(N_MOD,),
        in_specs=[pl.BlockSpec((8, D_MODEL), lambda j: (0, 0)),
                  pl.BlockSpec((D_MODEL, tn), lambda j: (0, j)),
                  pl.BlockSpec((1, tn), lambda j: (0, j))],
        out_specs=pl.BlockSpec((8, tn), lambda j: (0, j)),
        out_shape=jax.ShapeDtypeStruct((8, N_MOD * D_MODEL), F32),
        compiler_params=_cparams(("arbitrary",)),
        name="modulation",
    )(cvec8, ada_w, ada_b.reshape(1, -1))
    return out.reshape(8, N_MOD, D_MODEL)


def _mod_index(row0, tiles_per_batch):
    return lambda i: (row0 + i // tiles_per_batch, 0, 0)


def _in_kernel(x_ref, mod_ref, g_ref, wq_ref, bq_ref, sc_ref, wg_ref, bg_ref, wf_ref, bf_ref,
               qkvo_ref, gates_ref, u_ref):
    x = x_ref[...]
    ms = jnp.mean(x * x, axis=-1, keepdims=True)
    y = x * lax.rsqrt(ms + EPS) * g_ref[...]
    h = y * (1.0 + mod_ref[0, 1:2, :]) + mod_ref[0, 0:1, :]
    hb = h.astype(BF16)
    p = (_dot(hb, wq_ref[...]) + bq_ref[...]) * sc_ref[...]
    qkvo_ref[...] = p.astype(BF16)
    gates_ref[...] = _dot(hb, wg_ref[...]) + bg_ref[...]
    u_ref[...] = _dot(hb, wf_ref[...]) + bf_ref[...]


def _in_proj(x2, mod, row0, tiles_per_batch, tm, g1, wq, bq, sc, wg, bg, wf, bfo):
    n = x2.shape[0]
    wq_n = wq.shape[1]
    const = lambda i: (0, 0)
    return pl.pallas_call(
        _in_kernel,
        grid=(n // tm,),
        in_specs=[pl.BlockSpec((tm, D_MODEL), lambda i: (i, 0)),
                  pl.BlockSpec((1, N_MOD, D_MODEL), _mod_index(row0, tiles_per_batch)),
                  pl.BlockSpec((1, D_MODEL), const),
                  pl.BlockSpec((D_MODEL, wq_n), const), pl.BlockSpec((1, wq_n), const),
                  pl.BlockSpec((1, wq_n), const),
                  pl.BlockSpec((D_MODEL, LANES), const), pl.BlockSpec((1, LANES), const),
                  pl.BlockSpec((D_MODEL, FOURIER_WIDTH), const), pl.BlockSpec((1, FOURIER_WIDTH), const)],
        out_specs=[pl.BlockSpec((tm, wq_n), lambda i: (i, 0)),
                   pl.BlockSpec((tm, LANES), lambda i: (i, 0)),
                   pl.BlockSpec((tm, FOURIER_WIDTH), lambda i: (i, 0))],
        out_shape=[jax.ShapeDtypeStruct((n, wq_n), BF16),
                   jax.ShapeDtypeStruct((n, LANES), F32),
                   jax.ShapeDtypeStruct((n, FOURIER_WIDTH), F32)],
        compiler_params=_cparams(("arbitrary",), VMEM_LIMIT),
        name="in_proj",
    )(x2, mod, g1, wq, bq, sc, wg, bg, wf, bfo)


def _gates_kernel(g_ref, bc_ref, bct_ref):
    row = lax.broadcasted_iota(I32, (CHUNK, CHUNK), 0)
    col = lax.broadcasted_iota(I32, (CHUNK, CHUNK), 1)
    tril = (col <= row).astype(BF16)
    triu = (col >= row).astype(BF16)
    lane = lax.broadcasted_iota(I32, (CHUNK, LANES), 1)
    for c in range(bct_ref.shape[0]):
        sl = pl.ds(c * CHUNK, CHUNK)
        g = g_ref[sl, :]
        ls = jnp.minimum(g, 0.0) - jnp.log(1.0 + jnp.exp(-jnp.abs(g)))
        l1, l2, l3 = _split3(ls)
        pre = _dot(tril, l1) + _dot(tril, l2) + _dot(tril, l3)
        suf = _dot(triu, l1) + _dot(triu, l2) + _dot(triu, l3)
        bc = jnp.where((lane >= 4) & (lane < 8), pre,
                       jnp.where((lane >= 12) & (lane < 16), suf, g))
        bc_ref[sl, :] = bc
        bct_ref[c] = bc.T


def _gate_sums(gates, chunks_per_step=8):
    n = gates.shape[0]
    nc = n // CHUNK
    cps = chunks_per_step
    return pl.pallas_call(
        _gates_kernel,
        grid=(nc // cps,),
        in_specs=[pl.BlockSpec((cps * CHUNK, LANES), lambda i: (i, 0))],
        out_specs=[pl.BlockSpec((cps * CHUNK, LANES), lambda i: (i, 0)),
                   pl.BlockSpec((cps, LANES, CHUNK), lambda i: (i, 0, 0))],
        out_shape=[jax.ShapeDtypeStruct((n, LANES), F32),
                   jax.ShapeDtypeStruct((nc, LANES, CHUNK), F32)],
        compiler_params=_cparams(("arbitrary",)),
        name="gate_sums",
    )(gates)


def _mlstm_kernel(q_ref, k_ref, v_ref, o_ref, bc_ref, bct_ref, c0_ref, n0_ref, m0_ref, hg_ref,
                  hm_ref, cf_ref, nf_ref, mf_ref, hf_s, hb_s, c_s, n_s, m_s, *, n_chunks):
    head = pl.program_id(1)
    c_s[...] = c0_ref[0, :, 0]
    n_s[...] = n0_ref[0, :, 0]
    m_s[...] = m0_ref[0, :, 0]
    row = lax.broadcasted_iota(I32, (CHUNK, CHUNK), 0)
    col = lax.broadcasted_iota(I32, (CHUNK, CHUNK), 1)
    lane = lax.broadcasted_iota(I32, (CHUNK, LANES), 1)
    sub = lax.broadcasted_iota(I32, (LANES, CHUNK), 0)
    nt = (((1,), (1,)), ((), ()))
    tn = (((0,), (0,)), ((), ()))

    def one_direction(c, d, h_out):
        sl = pl.ds(pl.multiple_of(c * CHUNK, CHUNK), CHUNK)
        q = q_ref[0, sl, :]
        k = k_ref[0, sl, :]
        v = v_ref[0, sl, :]
        bc = bc_ref[0, sl, :]
        bct = bct_ref[0, c]
        ich = 8 * d + head
        fch = 8 * d + 4 + head
        ig_col = jnp.sum(jnp.where(lane == ich, bc, 0.0), axis=1, keepdims=True)
        b_col = jnp.sum(jnp.where(lane == fch, bc, 0.0), axis=1, keepdims=True)
        ig_row = jnp.sum(jnp.where(sub == ich, bct, 0.0), axis=0, keepdims=True)
        b_row = jnp.sum(jnp.where(sub == fch, bct, 0.0), axis=0, keepdims=True)
        if d == 0:
            g = b_col[CHUNK - 1:CHUNK, :]
            mask = col <= row
        else:
            g = b_col[0:1, :]
            mask = col >= row
        c_prev = c_s[d]
        n_prev = n_s[d]
        m_prev = m_s[d][:, 0:1]
        w_log = g - b_col + ig_col
        m_loc = jnp.max(w_log, axis=0, keepdims=True)
        w = jnp.exp(w_log - m_loc)
        wk = w * k.astype(F32)
        c_loc = lax.dot_general(v, wk.astype(BF16), tn, preferred_element_type=F32)
        n_loc = jnp.sum(wk, axis=0, keepdims=True)
        dmat = jnp.where(mask, b_col - b_row + ig_row, -jnp.inf)
        inter_log = b_col + m_prev
        m_comb = jnp.maximum(inter_log, jnp.max(dmat, axis=1, keepdims=True))
        s = lax.dot_general(q, k, nt, preferred_element_type=F32) * jnp.exp(dmat - m_comb)
        w_inter = jnp.exp(inter_log - m_comb)
        cq = lax.dot_general(q, c_prev.astype(BF16), nt, preferred_element_type=F32)
        num = _dot(s.astype(BF16), v) + w_inter * cq
        qn = jnp.sum(q.astype(F32) * n_prev, axis=1, keepdims=True)
        den = jnp.sum(s, axis=1, keepdims=True) + w_inter * qn
        h_out[sl, :] = num / jnp.maximum(jnp.abs(den), jnp.exp(-m_comb))
        m_new = jnp.maximum(g + m_prev, m_loc)
        a = jnp.exp(g + m_prev - m_new)
        bb = jnp.exp(m_loc - m_new)
        c_s[d] = a * c_prev + bb * c_loc
        n_s[d] = a * n_prev + bb * n_loc
        m_s[d] = jnp.broadcast_to(m_new, (1, LANES))

    def scan_body(i, carry):
        one_direction(i, 0, hf_s)
        one_direction(n_chunks - 1 - i, 1, hb_s)
        return carry

    lax.fori_loop(0, n_chunks, scan_body, 0)

    def out_body(c, carry):
        sl = pl.ds(pl.multiple_of(c * CHUNK, CHUNK), CHUNK)
        hs = hf_s[sl, :] + hb_s[sl, :]
        hn = hs * lax.rsqrt(jnp.mean(hs * hs, axis=-1, keepdims=True) + EPS) * hg_ref[0]
        hm_ref[0, sl, :] = (hn * jax.nn.sigmoid(o_ref[0, sl, :].astype(F32))).astype(BF16)
        return carry

    lax.fori_loop(0, n_chunks, out_body, 0)
    cf_ref[0, :, 0] = c_s[...]
    nf_ref[0, :, 0] = n_s[...]
    mf_ref[0, :, 0] = m_s[...]


def _mlstm(qkvo3, bc3, bct4, c0, n0, m0, head_g3):
    b, t, _ = qkvo3.shape
    nc = t // CHUNK
    blk = lambda off: pl.BlockSpec((1, t, HEAD_DIM), lambda i, h: (i, 0, off + h))
    st_c = pl.BlockSpec((1, 2, 1, HEAD_DIM, HEAD_DIM), lambda i, h: (i, 0, h, 0, 0))
    st_v = pl.BlockSpec((1, 2, 1, 1, LANES), lambda i, h: (i, 0, h, 0, 0))
    return pl.pallas_call(
        functools.partial(_mlstm_kernel, n_chunks=nc),
        grid=(b, N_HEADS),
        in_specs=[blk(0), blk(N_HEADS), blk(2 * N_HEADS), blk(3 * N_HEADS),
                  pl.BlockSpec((1, t, LANES), lambda i, h: (i, 0, 0)),
                  pl.BlockSpec((1, nc, LANES, CHUNK), lambda i, h: (i, 0, 0, 0)),
                  st_c, st_v, st_v,
                  pl.BlockSpec((1, 1, HEAD_DIM), lambda i, h: (h, 0, 0))],
        out_specs=[pl.BlockSpec((1, t, HEAD_DIM), lambda i, h: (i, 0, h)), st_c, st_v, st_v],
        out_shape=[jax.ShapeDtypeStruct((b, t, MLSTM_WIDTH), BF16),
                   jax.ShapeDtypeStruct((b, 2, N_HEADS, HEAD_DIM, HEAD_DIM), F32),
                   jax.ShapeDtypeStruct((b, 2, N_HEADS, 1, LANES), F32),
                   jax.ShapeDtypeStruct((b, 2, N_HEADS, 1, LANES), F32)],
        scratch_shapes=[pltpu.VMEM((t, HEAD_DIM), F32), pltpu.VMEM((t, HEAD_DIM), F32),
                        pltpu.VMEM((2, HEAD_DIM, HEAD_DIM), F32), pltpu.VMEM((2, 1, LANES), F32),
                        pltpu.VMEM((2, 1, LANES), F32)],
        compiler_params=_cparams(("arbitrary", "arbitrary"), VMEM_LIMIT),
        name="mlstm",
    )(qkvo3, qkvo3, qkvo3, qkvo3, bc3, bct4, c0, n0, m0, head_g3)


def _dft_cos_sin(n):
    k = np.arange(n)
    ang = 2.0 * np.pi * ((k[:, None] * k[None, :]) % n) / n
    return np.cos(ang), np.sin(ang)


def _channel_dft(scale):
    cd, sd = _dft_cos_sin(GROUP_DIM)
    eye = np.eye(N_GROUPS)
    return (jnp.asarray(np.kron(eye, cd) * scale, F32), jnp.asarray(np.kron(eye, -sd) * scale, F32))


def _fourier_ctx_kernel(u_ref, bdc_ref, bds_ref, ct_ref, st_ref, o_ref):
    u = u_ref[0]
    a = _dot3(u, bdc_ref[...])
    b = _dot3(u, bds_ref[...])
    o_ref[0] = (_dot3(ct_ref[...], a) + _dot3(st_ref[...], b)).astype(BF16)


def _fourier_ctx(u3):
    b, t, w = u3.shape
    bdc, bds = _channel_dft(1.0 / math.sqrt(t * GROUP_DIM))
    ct, st = _dft_cos_sin(t)
    const = lambda i: (0, 0)
    return pl.pallas_call(
        _fourier_ctx_kernel,
        grid=(b,),
        in_specs=[pl.BlockSpec((1, t, w), lambda i: (i, 0, 0)),
                  pl.BlockSpec((w, w), const), pl.BlockSpec((w, w), const),
                  pl.BlockSpec((t, t), const), pl.BlockSpec((t, t), const)],
        out_specs=pl.BlockSpec((1, t, w), lambda i: (i, 0, 0)),
        out_shape=jax.ShapeDtypeStruct((b, t, w), BF16),
        compiler_params=_cparams(("arbitrary",)),
        name="fourier_ctx",
    )(u3, bdc, bds, jnp.asarray(ct, F32), jnp.asarray(st, F32))


def _fourier_chan_kernel(u_ref, bdc_ref, bds_ref, a_ref, b_ref):
    u = u_ref[0]
    a_ref[0] = _dot3(u, bdc_ref[...]).astype(BF16)
    b_ref[0] = _dot3(u, bds_ref[...]).astype(BF16)


def _fourier_pos_kernel(cre_ref, sre_ref, cct_ref, sct_ref, a_ref, b_ref, o_ref, cp_s, sp_s):
    @pl.when(pl.program_id(1) == 0)
    def _():
        gw = cct_ref.shape[0]
        for q in range(cre_ref.shape[0]):
            cr, sr = cre_ref[q], sre_ref[q]
            cc, sc = cct_ref[...], sct_ref[...]
            cp_s[q * gw:(q + 1) * gw, :] = (cr * cc - sr * sc).astype(BF16)
            sp_s[q * gw:(q + 1) * gw, :] = (sr * cc + cr * sc).astype(BF16)

    o_ref[0] = (_dot(cp_s[...], a_ref[0]) + _dot(sp_s[...], b_ref[0])).astype(BF16)


def _fourier_lat(u3, grid_w):
    b, t, w = u3.shape
    rows = t // grid_w
    bdc, bds = _channel_dft(1.0 / math.sqrt(t * GROUP_DIM))
    tm = 512
    const = lambda i, j: (0, 0)
    a, bm = pl.pallas_call(
        _fourier_chan_kernel,
        grid=(b, t // tm),
        in_specs=[pl.BlockSpec((1, tm, w), lambda i, j: (i, j, 0)),
                  pl.BlockSpec((w, w), const), pl.BlockSpec((w, w), const)],
        out_specs=[pl.BlockSpec((1, tm, w), lambda i, j: (i, j, 0))] * 2,
        out_shape=[jax.ShapeDtypeStruct((b, t, w), BF16)] * 2,
        compiler_params=_cparams(("arbitrary", "arbitrary")),
        name="fourier_chan",
    )(u3, bdc, bds)
    cr, sr = _dft_cos_sin(rows)
    cc, sc = _dft_cos_sin(grid_w)
    cre = jnp.asarray(np.repeat(cr, grid_w, axis=1)[:, None, :], F32)
    sre = jnp.asarray(np.repeat(sr, grid_w, axis=1)[:, None, :], F32)
    cct = jnp.asarray(np.tile(cc, (1, rows)), F32)
    sct = jnp.asarray(np.tile(sc, (1, rows)), F32)
    rpt = tm // grid_w
    return pl.pallas_call(
        _fourier_pos_kernel,
        grid=(t // tm, b),
        in_specs=[pl.BlockSpec((rpt, 1, t), lambda j, i: (j, 0, 0)),
                  pl.BlockSpec((rpt, 1, t), lambda j, i: (j, 0, 0)),
                  pl.BlockSpec((grid_w, t), lambda j, i: (0, 0)),
                  pl.BlockSpec((grid_w, t), lambda j, i: (0, 0)),
                  pl.BlockSpec((1, t, w), lambda j, i: (i, 0, 0)),
                  pl.BlockSpec((1, t, w), lambda j, i: (i, 0, 0))],
        out_specs=pl.BlockSpec((1, tm, w), lambda j, i: (i, j, 0)),
        out_shape=jax.ShapeDtypeStruct((b, t, w), BF16),
        scratch_shapes=[pltpu.VMEM((tm, t), BF16), pltpu.VMEM((tm, t), BF16)],
        compiler_params=_cparams(("arbitrary", "arbitrary"), VMEM_LIMIT),
        name="fourier_pos",
    )(cre, sre, cct, sct, a, bm)


def _out_kernel(hm_ref, fu_ref, x_ref, mod_ref, g_ref, wa_ref, wb_ref, rw_ref,
                x1_ref, aug_ref, afft_ref):
    a = _dot(hm_ref[...], wa_ref[...]) + _dot(fu_ref[...], wb_ref[...])
    x1 = x_ref[...] + mod_ref[0, 2:3, :] * a
    x1_ref[...] = x1
    ms = jnp.mean(x1 * x1, axis=-1, keepdims=True)
    y = x1 * lax.rsqrt(ms + EPS) * g_ref[...]
    h2 = y * (1.0 + mod_ref[0, 4:5, :]) + mod_ref[0, 3:4, :]
    logits = _dot3(h2, rw_ref[...])
    lane = lax.broadcasted_iota(I32, logits.shape, 1)
    valid = lane < N_EXPERTS
    lg = jnp.where(valid, logits, -1e30)
    ex = jnp.where(valid, jnp.exp(lg - jnp.max(lg, axis=1, keepdims=True)), 0.0)
    aff = ex / jnp.sum(ex, axis=1, keepdims=True)
    aug_ref[:, :D_MODEL] = h2
    aug_ref[:, D_MODEL:] = aff
    afft_ref[...] = aff.T[:N_EXPERTS, :]


def _out_proj(hm2, fu2, x2, mod, row0, tiles_per_batch, tm, g2, wa, wb, rw):
    n = x2.shape[0]
    const = lambda i: (0, 0)
    return pl.pallas_call(
        _out_kernel,
        grid=(n // tm,),
        in_specs=[pl.BlockSpec((tm, MLSTM_WIDTH), lambda i: (i, 0)),
                  pl.BlockSpec((tm, FOURIER_WIDTH), lambda i: (i, 0)),
                  pl.BlockSpec((tm, D_MODEL), lambda i: (i, 0)),
                  pl.BlockSpec((1, N_MOD, D_MODEL), _mod_index(row0, tiles_per_batch)),
                  pl.BlockSpec((1, D_MODEL), const),
                  pl.BlockSpec((MLSTM_WIDTH, D_MODEL), const),
                  pl.BlockSpec((FOURIER_WIDTH, D_MODEL), const),
                  pl.BlockSpec((D_MODEL, LANES), const)],
        out_specs=[pl.BlockSpec((tm, D_MODEL), lambda i: (i, 0)),
                   pl.BlockSpec((tm, AUG), lambda i: (i, 0)),
                   pl.BlockSpec((N_EXPERTS, tm), lambda i: (0, i))],
        out_shape=[jax.ShapeDtypeStruct((n, D_MODEL), F32),
                   jax.ShapeDtypeStruct((n, AUG), F32),
                   jax.ShapeDtypeStruct((N_EXPERTS, n), F32)],
        compiler_params=_cparams(("arbitrary",), VMEM_LIMIT),
        name="out_proj",
    )(hm2, fu2, x2, mod, g2, wa, wb, rw)


def _topk_kernel(aff_ref, pos_ref, blk_ref, *, n, cap):
    nb = n // LANES
    aff = aff_ref[...]

    def enough(t):
        return jnp.sum(jnp.where(aff >= t, 1.0, 0.0), axis=1, keepdims=True) >= cap

    def pow2(k):
        return lax.bitcast_convert_type(jnp.left_shift(127 - k, 23), F32)

    def exp_search(_, c):
        lo, hi = c
        mid = jnp.right_shift(lo + hi, 1)
        ok = enough(pow2(jnp.minimum(mid, 126)))
        return jnp.where(ok, lo, mid + 1), jnp.where(ok, mid, hi)

    kz = jnp.zeros((N_EXPERTS, 1), I32)
    kstar, _ = lax.fori_loop(0, 7, exp_search, (kz, kz + 127))
    found = kstar < 127
    p = pow2(jnp.minimum(kstar, 126))
    t_lo0 = jnp.where(found, p, 0.0)
    t_hi0 = jnp.where(found, 2.0 * p, p)

    def bisect(_, c):
        t_lo, t_hi = c
        mid = 0.5 * (t_lo + t_hi)
        ok = enough(mid)
        return jnp.where(ok, mid, t_lo), jnp.where(ok, t_hi, mid)

    t_lo, t_hi = lax.fori_loop(0, 40, bisect, (t_lo0, t_hi0))
    n_gt = jnp.sum(jnp.where(aff >= t_hi, 1.0, 0.0), axis=1, keepdims=True)
    need = cap - n_gt
    row = lax.broadcasted_iota(I32, (LANES, LANES), 0)
    col = lax.broadcasted_iota(I32, (LANES, LANES), 1)
    upper = (row <= col).astype(BF16)
    blane = lax.broadcasted_iota(I32, blk_ref.shape, 1)

    def block(b, carry):
        eq_off, sel_off = carry
        sl = pl.ds(pl.multiple_of(b * LANES, LANES), LANES)
        ab = aff_ref[:, sl]
        gt = ab >= t_hi
        eq = (ab >= t_lo) & (ab < t_hi)
        eqf = jnp.where(eq, 1.0, 0.0)
        eq_rank = _dot(eqf.astype(BF16), upper) + eq_off - eqf
        sel = gt | (eq & (eq_rank < need))
        self_ = jnp.where(sel, 1.0, 0.0)
        cum = _dot(self_.astype(BF16), upper) + sel_off
        pos_ref[:, sl] = jnp.where(sel, cum - 1.0, -1.0).astype(I32)
        blk_ref[...] = jnp.where(blane == b, sel_off.astype(I32), blk_ref[...])
        return (eq_off + jnp.sum(eqf, axis=1, keepdims=True),
                sel_off + jnp.sum(self_, axis=1, keepdims=True))

    blk_ref[...] = jnp.full(blk_ref.shape, cap, I32)
    zero = jnp.zeros((N_EXPERTS, 1), F32)
    lax.fori_loop(0, nb, block, (zero, zero))


def _topk(aff_t, cap):
    n = aff_t.shape[1]
    return pl.pallas_call(
        functools.partial(_topk_kernel, n=n, cap=cap),
        grid=(1,),
        in_specs=[pl.BlockSpec((N_EXPERTS, n), lambda i: (0, 0))],
        out_specs=[pl.BlockSpec((N_EXPERTS, n), lambda i: (0, 0)),
                   pl.BlockSpec((N_EXPERTS, 2 * LANES), lambda i: (0, 0))],
        out_shape=[jax.ShapeDtypeStruct((N_EXPERTS, n), I32),
                   jax.ShapeDtypeStruct((N_EXPERTS, 2 * LANES), I32)],
        compiler_params=_cparams(("arbitrary",)),
        name="topk",
    )(aff_t)


def _compact_kernel(blk_sm, pos_ref, out_ref, *, nb):
    e = pl.program_id(0)

    def count_le(x):
        lo, hi = jnp.int32(0), jnp.int32(nb + 1)
        for _ in range(9):
            mid = (lo + hi) >> 1
            le = (blk_sm[e, mid] <= x) & (lo < hi)
            lo, hi = jnp.where(le, mid + 1, lo), jnp.where(le | (lo >= hi), hi, mid)
        return lo

    slot0 = lax.broadcasted_iota(I32, (LANES, LANES), 0)
    tok = lax.broadcasted_iota(I32, (LANES, LANES), 1)

    def chunk(jc, carry):
        j0 = jc * LANES
        first = count_le(j0) - 1
        last = jnp.minimum(count_le(j0 + LANES - 1), nb)
        slot = slot0 + j0

        def body(b, acc):
            prow = pos_ref[0, :, pl.ds(pl.multiple_of(b * LANES, LANES), LANES)]
            return acc + jnp.where(prow == slot, (tok + b * LANES).astype(F32), 0.0)

        acc = lax.fori_loop(first, last, body, jnp.zeros((LANES, LANES), F32))
        out_ref[0, jc] = jnp.sum(acc.T, axis=0, keepdims=True).astype(I32)
        return carry

    lax.fori_loop(0, out_ref.shape[1], chunk, 0)


def _compact(pos, blk, cap):
    n = pos.shape[1]
    ncj = cap // LANES
    out = pl.pallas_call(
        functools.partial(_compact_kernel, nb=n // LANES),
        grid_spec=pltpu.PrefetchScalarGridSpec(
            num_scalar_prefetch=1,
            grid=(N_EXPERTS,),
            in_specs=[pl.BlockSpec((1, 1, n), lambda e, blk: (e, 0, 0))],
            out_specs=pl.BlockSpec((1, ncj, 1, LANES), lambda e, blk: (e, 0, 0, 0)),
        ),
        out_shape=jax.ShapeDtypeStruct((N_EXPERTS, ncj, 1, LANES), I32),
        compiler_params=_cparams(("arbitrary",)),
        name="compact",
    )(blk, pos.reshape(N_EXPERTS, 1, n))
    return out.reshape(N_EXPERTS, cap)


def _moe_kernel(lists_sm, aug_c, aug_l, w1_ref, w3_ref, w2_ref, ye_ref,
                xbuf, xe, gate, w1b, w3b, w2b, sem, *, cap_c, cap_l, row_chunk, unroll):
    e = pl.program_id(0)
    f = pl.program_id(1)
    ne = pl.num_programs(0)
    nf = pl.num_programs(1)
    rows = cap_c + cap_l

    def gather_rows(expert):
        base = expert * rows

        def body_for(aug):
            def body(g, carry):
                for u in range(unroll):
                    j = g * unroll + u
                    pltpu.make_async_copy(aug.at[pl.ds(lists_sm[base + j], 1)], xbuf.at[pl.ds(j, 1)],
                                          sem.at[0]).start()
                return carry
            return body

        lax.fori_loop(0, cap_c // unroll, body_for(aug_c), 0)
        lax.fori_loop(cap_c // unroll, rows // unroll, body_for(aug_l), 0)

    @pl.when((e == 0) & (f == 0))
    def _():
        gather_rows(e)

    @pl.when(f == 0)
    def _():
        pltpu.make_async_copy(xbuf, xbuf, sem.at[0]).wait()
        xe[...] = xbuf[:, :D_MODEL].astype(BF16)
        lane = lax.broadcasted_iota(I32, (rows, LANES), 1)
        gate[...] = jnp.sum(jnp.where(lane == e, xbuf[:, D_MODEL:], 0.0), axis=1, keepdims=True)
        ye_ref[0, rows:, :] = jnp.zeros((ye_ref.shape[1] - rows, D_MODEL), F32)

        @pl.when(e + 1 < ne)
        def _():
            gather_rows(e + 1)

    w1b[...] = w1_ref[0].astype(BF16)
    w3b[...] = w3_ref[0].astype(BF16)
    w2b[...] = w2_ref[0].astype(BF16)
    for r in range(rows // row_chunk):
        sl = pl.ds(r * row_chunk, row_chunk)
        x = xe[sl, :]
        a = _dot(x, w1b[...])
        bg = _dot(x, w3b[...])
        act = (a * jax.nn.sigmoid(a) * bg).astype(BF16)
        part = _dot(act, w2b[...])

        @pl.when(f == 0)
        def _():
            ye_ref[0, sl, :] = part

        @pl.when((f > 0) & (f < nf - 1))
        def _():
            ye_ref[0, sl, :] += part

        @pl.when(f == nf - 1)
        def _():
            ye_ref[0, sl, :] = (ye_ref[0, sl, :] + part) * gate[sl, :]


def _moe(lists, aug_c, aug_l, w1, w3, w2, cap_c, cap_l, tf=256, row_chunk=512, unroll=8):
    rows = cap_c + cap_l
    any_spec = pl.BlockSpec(memory_space=pl.ANY)
    return pl.pallas_call(
        functools.partial(_moe_kernel, cap_c=cap_c, cap_l=cap_l, row_chunk=row_chunk, unroll=unroll),
        grid_spec=pltpu.PrefetchScalarGridSpec(
            num_scalar_prefetch=1,
            grid=(N_EXPERTS, D_EXPERT // tf),
            in_specs=[any_spec, any_spec,
                      pl.BlockSpec((1, D_MODEL, tf), lambda e, f, ls: (e, 0, f)),
                      pl.BlockSpec((1, D_MODEL, tf), lambda e, f, ls: (e, 0, f)),
                      pl.BlockSpec((1, tf, D_MODEL), lambda e, f, ls: (e, f, 0))],
            out_specs=pl.BlockSpec((1, rows + WINDOW, D_MODEL), lambda e, f, ls: (e, 0, 0)),
            scratch_shapes=[pltpu.VMEM((rows, AUG), F32), pltpu.VMEM((rows, D_MODEL), BF16),
                            pltpu.VMEM((rows, 1), F32),
                            pltpu.VMEM((D_MODEL, tf), BF16), pltpu.VMEM((D_MODEL, tf), BF16),
                            pltpu.VMEM((tf, D_MODEL), BF16), pltpu.SemaphoreType.DMA((1,))],
        ),
        out_shape=jax.ShapeDtypeStruct((N_EXPERTS, rows + WINDOW, D_MODEL), F32),
        compiler_params=_cparams(("arbitrary", "arbitrary"), VMEM_LIMIT),
        name="moe",
    )(lists, aug_c, aug_l, w1, w3, w2)


def _combine_kernel(seg_sm, x1_ref, pos_ref, mod_ref, g_ref, ye_hbm, o_ref, wbuf, sem, *, base, tm):
    i = pl.program_id(0)
    nt = pl.num_programs(0)
    bpt = tm // LANES
    slot = lax.rem(i, 2)
    last_start = ye_hbm.shape[1] - WINDOW
    tn = (((0,), (0,)), ((), ()))

    def start_windows(tile, rnd, buf):
        for e in range(N_EXPERTS):
            start = jnp.minimum(base + (seg_sm[e, tile * bpt] & -SUBLANES) + rnd * WINDOW, last_start)
            pltpu.make_async_copy(ye_hbm.at[e, pl.ds(pl.multiple_of(start, SUBLANES), WINDOW)],
                                  wbuf.at[buf, pl.ds(e * WINDOW, WINDOW)], sem.at[buf]).start()

    def wait_windows(buf):
        pltpu.make_async_copy(wbuf.at[buf], wbuf.at[buf], sem.at[buf]).wait()

    @pl.when(i == 0)
    def _():
        start_windows(i, 0, slot)

    @pl.when(i + 1 < nt)
    def _():
        start_windows(i + 1, 0, 1 - slot)

    most = jnp.int32(0)
    for e in range(N_EXPERTS):
        most = jnp.maximum(most, seg_sm[e, (i + 1) * bpt] - (seg_sm[e, i * bpt] & -SUBLANES))
    n_rounds = jnp.right_shift(most + (WINDOW - 1), WINDOW_SHIFT)

    pos = pos_ref[...]
    valid = pos >= 0
    first = jnp.min(jnp.where(valid, pos, jnp.int32(2 ** 30)), axis=1, keepdims=True)
    rel = pos - (first & -SUBLANES)
    lane = lax.broadcasted_iota(I32, (N_EXPERTS, N_EXPERTS * WINDOW), 1)
    owner = lax.broadcasted_iota(I32, (N_EXPERTS, N_EXPERTS * WINDOW), 0)
    spread = jnp.where(jnp.right_shift(lane, WINDOW_SHIFT) == owner, 1.0, 0.0).astype(BF16)
    wrow = (lax.broadcasted_iota(I32, (1, N_EXPERTS * WINDOW), 1) & (WINDOW - 1)).astype(F32)

    def place(rnd, y):
        q = jnp.where(valid & (rel >= rnd * WINDOW) & (rel < (rnd + 1) * WINDOW), rel - rnd * WINDOW, -1)
        qb = lax.dot_general(q.astype(F32).astype(BF16), spread, tn, preferred_element_type=F32)
        onehot = jnp.where(qb == wrow, 1.0, 0.0).astype(BF16)
        return y + _dot(onehot, wbuf[slot].astype(BF16))

    wait_windows(slot)
    y = place(0, jnp.zeros((tm, D_MODEL), F32))

    def extra_round(rnd, y):
        start_windows(i, rnd, slot)
        wait_windows(slot)
        return place(rnd, y)

    y = lax.fori_loop(1, n_rounds, extra_round, y)
    x = x1_ref[...] + mod_ref[0, 5:6, :] * y
    ms = jnp.mean(x * x, axis=-1, keepdims=True)
    o_ref[...] = x * lax.rsqrt(ms + EPS) * g_ref[...]


def _combine_norm(seg, x1, pos, ye, base, mod, row0, tiles_per_batch, tm, gf):
    n = x1.shape[0]
    return pl.pallas_call(
        functools.partial(_combine_kernel, base=base, tm=tm),
        grid_spec=pltpu.PrefetchScalarGridSpec(
            num_scalar_prefetch=1,
            grid=(n // tm,),
            in_specs=[pl.BlockSpec((tm, D_MODEL), lambda i, sg: (i, 0)),
                      pl.BlockSpec((N_EXPERTS, tm), lambda i, sg: (0, i)),
                      pl.BlockSpec((1, N_MOD, D_MODEL),
                                   lambda i, sg: (row0 + i // tiles_per_batch, 0, 0)),
                      pl.BlockSpec((1, D_MODEL), lambda i, sg: (0, 0)),
                      pl.BlockSpec(memory_space=pl.ANY)],
            out_specs=pl.BlockSpec((tm, D_MODEL), lambda i, sg: (i, 0)),
            scratch_shapes=[pltpu.VMEM((2, N_EXPERTS * WINDOW, D_MODEL), F32),
                            pltpu.SemaphoreType.DMA((2,))],
        ),
        out_shape=jax.ShapeDtypeStruct((n, D_MODEL), F32),
        compiler_params=_cparams(("arbitrary",), VMEM_LIMIT),
        name="combine_norm",
    )(seg, x1, pos, mod, gf, ye)


def kernel(x_prompt, x_sample, state_C, state_n, state_m, c, c_ctx, ada_w, ada_b, norm1_g, norm2_g,
           w_in, b_in, head_g, w_out, router_w, exp_w1, exp_w3, exp_w2, final_g):
    depth = ada_w.shape[0]
    assert depth == 1, "single-layer trunk"
    bc_, tc_, _ = x_prompt.shape
    bl_, tl_, _ = x_sample.shape
    w4 = 4 * MLSTM_WIDTH
    tm = 512

    cvec = jnp.concatenate([c_ctx[None, :], c, jnp.zeros((8 - 1 - bl_, D_MODEL), F32)], axis=0)
    mod = _modulation(cvec, ada_w[0], ada_b[0])

    wi = w_in[0]
    bi = b_in[0]
    wq = wi[:, :w4].astype(BF16)
    bq = bi[None, :w4]
    wg = jnp.pad(wi[:, w4:w4 + 16], ((0, 0), (0, LANES - 16))).astype(BF16)
    bg = jnp.pad(bi[w4:w4 + 16], (0, LANES - 16))[None, :]
    wf = wi[:, w4 + 16:].astype(BF16)
    bfo = bi[None, w4 + 16:]
    colscale = jnp.ones((1, w4), F32).at[:, MLSTM_WIDTH:2 * MLSTM_WIDTH].set(HEAD_DIM ** -0.5)
    g1 = norm1_g[0][None, :]
    g2 = norm2_g[0][None, :]
    wo = w_out[0].astype(BF16)
    wa, wb = wo[:MLSTM_WIDTH], wo[MLSTM_WIDTH:]
    rw = jnp.pad(router_w[0], ((0, 0), (0, LANES - N_EXPERTS)))
    hg3 = head_g[0][:, None, :]

    def spread(v):
        return jnp.broadcast_to(v[..., None, None], v.shape + (1, LANES))

    def mixer_and_router(x3, row0, init, grid_w):
        b, t, _ = x3.shape
        x2 = x3.reshape(b * t, D_MODEL)
        tpb = max(t // tm, 1) if row0 else (b * t) // tm + 1
        qkvo, gates, u = _in_proj(x2, mod, row0, tpb, tm, g1, wq, bq, colscale, wg, bg, wf, bfo)
        bcs, bct = _gate_sums(gates)
        c0, n0, m0 = init
        hm, cf, nf, mf = _mlstm(qkvo.reshape(b, t, w4), bcs.reshape(b, t, LANES),
                                bct.reshape(b, t // CHUNK, LANES, CHUNK), c0, n0, m0, hg3)
        u3 = u.reshape(b, t, FOURIER_WIDTH)
        fu = _fourier_ctx(u3) if grid_w is None else _fourier_lat(u3, grid_w)
        x1, aug, aff_t = _out_proj(hm.reshape(b * t, MLSTM_WIDTH), fu.reshape(b * t, FOURIER_WIDTH),
                                   x2, mod, row0, tpb, tm, g2, wa, wb, rw)
        cap = CAPACITY_FACTOR * (b * t) // N_EXPERTS
        pos, blk = _topk(aff_t, cap)
        return x1, aug, _compact(pos, blk, cap), cap, (cf, nf, mf), (pos, blk)

    zero_init = (jnp.zeros((bc_, 2, N_HEADS, HEAD_DIM, HEAD_DIM), F32),
                 jnp.zeros((bc_, 2, N_HEADS, 1, LANES), F32),
                 jnp.zeros((bc_, 2, N_HEADS, 1, LANES), F32))
    lat_init = (state_C[:, 0], state_n[:, 0][..., None, :], spread(state_m[:, 0]))
    x1c, augc, listc, capc, (cf, nf, mf), (posc, blkc) = mixer_and_router(x_prompt, 0, zero_init, None)
    x1l, augl, listl, capl, _, (posl, blkl) = mixer_and_router(x_sample, 1, lat_init, GRID_W)

    lists = jnp.concatenate([listc, listl], axis=1).reshape(-1)
    ye = _moe(lists, augc, augl, exp_w1[0], exp_w3[0], exp_w2[0], capc, capl)
    gf = final_g[None, :]
    tc = 256
    y_prompt = _combine_norm(blkc, x1c, posc, ye, 0, mod, 0, bc_ * tc_ // tc + 1, tc, gf)
    y_sample = _combine_norm(blkl, x1l, posl, ye, capc, mod, 1, tl_ // tc, tc, gf)
    y_prompt = y_prompt.reshape(x_prompt.shape)
    y_sample = y_sample.reshape(x_sample.shape)
    new_c = cf[:, None]
    new_n = nf[:, None, :, :, 0, :]
    new_m = mf[:, None, :, :, 0, 0]
    return (y_prompt, y_sample, new_c, new_n, new_m)
```

```python
import functools
import math

import numpy as np
import jax
import jax.numpy as jnp
from jax import lax
from jax.experimental import pallas as pl
from jax.experimental.pallas import tpu as pltpu

F32 = jnp.float32
BF16 = jnp.bfloat16
I32 = jnp.int32

D_MODEL = 1024
N_HEADS = 4
HEAD_DIM = 128
MLSTM_WIDTH = N_HEADS * HEAD_DIM
FOURIER_WIDTH = 512
N_GROUPS = 4
GROUP_DIM = 128
CHUNK = 128
GRID_W = 64
N_EXPERTS = 16
CAPACITY_FACTOR = 2
D_EXPERT = 2048
N_MOD = 6
EPS = 1e-6
LANES = 128
ROW_ALIGN = 16
AUG = D_MODEL + LANES
WINDOW_SHIFT = 6
WINDOW = 1 << WINDOW_SHIFT
VMEM_LIMIT = 56 * 1024 * 1024


def _cparams(sem, vmem=None):
    return pltpu.CompilerParams(dimension_semantics=sem, vmem_limit_bytes=vmem)


def _split3(a):
    a1 = a.astype(BF16)
    r1 = a - a1.astype(F32)
    a2 = r1.astype(BF16)
    a3 = (r1 - a2.astype(F32)).astype(BF16)
    return a1, a2, a3


def _dot(a, b):
    return jnp.dot(a, b, preferred_element_type=F32)


def _dot3(a, b):
    a1, a2, _ = _split3(a)
    b1, b2, _ = _split3(b)
    return _dot(a1, b1) + (_dot(a1, b2) + _dot(a2, b1))


def _mod_kernel(c_ref, w_ref, b_ref, o_ref):
    c = c_ref[...]
    s = c * jax.nn.sigmoid(c)
    o_ref[...] = _dot(s.astype(BF16), w_ref[...].astype(BF16)) + b_ref[...]


def _modulation(cvec8, ada_w, ada_b):
    tn = D_MODEL
    out = pl.pallas_call(
        _mod_kernel,
        grid=(N_MOD,),
        in_specs=[pl.BlockSpec((8, D_MODEL), lambda j: (0, 0)),
                  pl.BlockSpec((D_MODEL, tn), lambda j: (0, j)),
                  pl.BlockSpec((1, tn), lambda j: (0, j))],
        out_specs=pl.BlockSpec((8, tn), lambda j: (0, j)),
        out_shape=jax.ShapeDtypeStruct((8, N_MOD * D_MODEL), F32),
        compiler_params=_cparams(("arbitrary",)),
        name="modulation",
    )(cvec8, ada_w, ada_b.reshape(1, -1))
    return out.reshape(8, N_MOD, D_MODEL)


def _mod_index(row0, tiles_per_batch):
    return lambda i: (row0 + i // tiles_per_batch, 0, 0)


def _in_kernel(x_ref, mod_ref, g_ref, wq_ref, bq_ref, sc_ref, wg_ref, bg_ref, wf_ref, bf_ref,
               qkvo_ref, gates_ref, u_ref):
    x = x_ref[...]
    ms = jnp.mean(x * x, axis=-1, keepdims=True)
    y = x * lax.rsqrt(ms + EPS) * g_ref[...]
    h = y * (1.0 + mod_ref[0, 1:2, :]) + mod_ref[0, 0:1, :]
    hb = h.astype(BF16)
    p = (_dot(hb, wq_ref[...]) + bq_ref[...]) * sc_ref[...]
    qkvo_ref[...] = p.astype(BF16)
    gates_ref[...] = _dot(hb, wg_ref[...]) + bg_ref[...]
    u_ref[...] = _dot(hb, wf_ref[...]) + bf_ref[...]


def _in_proj(x2, mod, row0, tiles_per_batch, tm, g1, wq, bq, sc, wg, bg, wf, bfo):
    n = x2.shape[0]
    wq_n = wq.shape[1]
    const = lambda i: (0, 0)
    return pl.pallas_call(
        _in_kernel,
        grid=(n // tm,),
        in_specs=[pl.BlockSpec((tm, D_MODEL), lambda i: (i, 0)),
                  pl.BlockSpec((1, N_MOD, D_MODEL), _mod_index(row0, tiles_per_batch)),
                  pl.BlockSpec((1, D_MODEL), const),
                  pl.BlockSpec((D_MODEL, wq_n), const), pl.BlockSpec((1, wq_n), const),
                  pl.BlockSpec((1, wq_n), const),
                  pl.BlockSpec((D_MODEL, LANES), const), pl.BlockSpec((1, LANES), const),
                  pl.BlockSpec((D_MODEL, FOURIER_WIDTH), const), pl.BlockSpec((1, FOURIER_WIDTH), const)],
        out_specs=[pl.BlockSpec((tm, wq_n), lambda i: (i, 0)),
                   pl.BlockSpec((tm, LANES), lambda i: (i, 0)),
                   pl.BlockSpec((tm, FOURIER_WIDTH), lambda i: (i, 0))],
        out_shape=[jax.ShapeDtypeStruct((n, wq_n), BF16),
                   jax.ShapeDtypeStruct((n, LANES), F32),
                   jax.ShapeDtypeStruct((n, FOURIER_WIDTH), F32)],
        compiler_params=_cparams(("arbitrary",), VMEM_LIMIT),
        name="in_proj",
    )(x2, mod, g1, wq, bq, sc, wg, bg, wf, bfo)


def _gates_kernel(g_ref, bc_ref, bct_ref):
    row = lax.broadcasted_iota(I32, (CHUNK, CHUNK), 0)
    col = lax.broadcasted_iota(I32, (CHUNK, CHUNK), 1)
    tril = (col <= row).astype(BF16)
    triu = (col >= row).astype(BF16)
    lane = lax.broadcasted_iota(I32, (CHUNK, LANES), 1)
    for c in range(bct_ref.shape[0]):
        sl = pl.ds(c * CHUNK, CHUNK)
        g = g_ref[sl, :]
        ls = jnp.minimum(g, 0.0) - jnp.log(1.0 + jnp.exp(-jnp.abs(g)))
        l1, l2, l3 = _split3(ls)
        pre = _dot(tril, l1) + _dot(tril, l2) + _dot(tril, l3)
        suf = _dot(triu, l1) + _dot(triu, l2) + _dot(triu, l3)
        bc = jnp.where((lane >= 4) & (lane < 8), pre,
                       jnp.where((lane >= 12) & (lane < 16), suf, g))
        bc_ref[sl, :] = bc
        bct_ref[c] = bc.T


def _gate_sums(gates, chunks_per_step=8):
    n = gates.shape[0]
    nc = n // CHUNK
    cps = chunks_per_step
    return pl.pallas_call(
        _gates_kernel,
        grid=(nc // cps,),
        in_specs=[pl.BlockSpec((cps * CHUNK, LANES), lambda i: (i, 0))],
        out_specs=[pl.BlockSpec((cps * CHUNK, LANES), lambda i: (i, 0)),
                   pl.BlockSpec((cps, LANES, CHUNK), lambda i: (i, 0, 0))],
        out_shape=[jax.ShapeDtypeStruct((n, LANES), F32),
                   jax.ShapeDtypeStruct((nc, LANES, CHUNK), F32)],
        compiler_params=_cparams(("arbitrary",)),
        name="gate_sums",
    )(gates)


def _mlstm_kernel(q_ref, k_ref, v_ref, o_ref, bc_ref, bct_ref, c0_ref, n0_ref, m0_ref, hg_ref,
                  hm_ref, cf_ref, nf_ref, mf_ref, hf_s, hb_s, c_s, n_s, m_s, *, n_chunks):
    head = pl.program_id(1)
    c_s[...] = c0_ref[0, :, 0]
    n_s[...] = n0_ref[0, :, 0]
    m_s[...] = m0_ref[0, :, 0]
    row = lax.broadcasted_iota(I32, (CHUNK, CHUNK), 0)
    col = lax.broadcasted_iota(I32, (CHUNK, CHUNK), 1)
    lane = lax.broadcasted_iota(I32, (CHUNK, LANES), 1)
    sub = lax.broadcasted_iota(I32, (LANES, CHUNK), 0)
    nt = (((1,), (1,)), ((), ()))
    tn = (((0,), (0,)), ((), ()))

    def one_direction(c, d, h_out):
        sl = pl.ds(pl.multiple_of(c * CHUNK, CHUNK), CHUNK)
        q = q_ref[0, sl, :]
        k = k_ref[0, sl, :]
        v = v_ref[0, sl, :]
        bc = bc_ref[0, sl, :]
        bct = bct_ref[0, c]
        ich = 8 * d + head
        fch = 8 * d + 4 + head
        ig_col = jnp.sum(jnp.where(lane == ich, bc, 0.0), axis=1, keepdims=True)
        b_col = jnp.sum(jnp.where(lane == fch, bc, 0.0), axis=1, keepdims=True)
        ig_row = jnp.sum(jnp.where(sub == ich, bct, 0.0), axis=0, keepdims=True)
        b_row = jnp.sum(jnp.where(sub == fch, bct, 0.0), axis=0, keepdims=True)
        if d == 0:
            g = b_col[CHUNK - 1:CHUNK, :]
            mask = col <= row
        else:
            g = b_col[0:1, :]
            mask = col >= row
        c_prev = c_s[d]
        n_prev = n_s[d]
        m_prev = m_s[d][:, 0:1]
        w_log = g - b_col + ig_col
        m_loc = jnp.max(w_log, axis=0, keepdims=True)
        w = jnp.exp(w_log - m_loc)
        wk = w * k.astype(F32)
        c_loc = lax.dot_general(v, wk.astype(BF16), tn, preferred_element_type=F32)
        n_loc = jnp.sum(wk, axis=0, keepdims=True)
        dmat = jnp.where(mask, b_col - b_row + ig_row, -jnp.inf)
        inter_log = b_col + m_prev
        m_comb = jnp.maximum(inter_log, jnp.max(dmat, axis=1, keepdims=True))
        s = lax.dot_general(q, k, nt, preferred_element_type=F32) * jnp.exp(dmat - m_comb)
        w_inter = jnp.exp(inter_log - m_comb)
        cq = lax.dot_general(q, c_prev.astype(BF16), nt, preferred_element_type=F32)
        num = _dot(s.astype(BF16), v) + w_inter * cq
        qn = jnp.sum(q.astype(F32) * n_prev, axis=1, keepdims=True)
        den = jnp.sum(s, axis=1, keepdims=True) + w_inter * qn
        h_out[sl, :] = num / jnp.maximum(jnp.abs(den), jnp.exp(-m_comb))
        m_new = jnp.maximum(g + m_prev, m_loc)
        a = jnp.exp(g + m_prev - m_new)
        bb = jnp.exp(m_loc - m_new)
        c_s[d] = a * c_prev + bb * c_loc
        n_s[d] = a * n_prev + bb * n_loc
        m_s[d] = jnp.broadcast_to(m_new, (1, LANES))

    def scan_body(i, carry):
        one_direction(i, 0, hf_s)
        one_direction(n_chunks - 1 - i, 1, hb_s)
        return carry

    lax.fori_loop(0, n_chunks, scan_body, 0)

    def out_body(c, carry):
        sl = pl.ds(pl.multiple_of(c * CHUNK, CHUNK), CHUNK)
        hs = hf_s[sl, :] + hb_s[sl, :]
        hn = hs * lax.rsqrt(jnp.mean(hs * hs, axis=-1, keepdims=True) + EPS) * hg_ref[0]
        hm_ref[0, sl, :] = (hn * jax.nn.sigmoid(o_ref[0, sl, :].astype(F32))).astype(BF16)
        return carry

    lax.fori_loop(0, n_chunks, out_body, 0)
    cf_ref[0, :, 0] = c_s[...]
    nf_ref[0, :, 0] = n_s[...]
    mf_ref[0, :, 0] = m_s[...]


def _mlstm(qkvo3, bc3, bct4, c0, n0, m0, head_g3):
    b, t, _ = qkvo3.shape
    nc = t // CHUNK
    blk = lambda off: pl.BlockSpec((1, t, HEAD_DIM), lambda i, h: (i, 0, off + h))
    st_c = pl.BlockSpec((1, 2, 1, HEAD_DIM, HEAD_DIM), lambda i, h: (i, 0, h, 0, 0))
    st_v = pl.BlockSpec((1, 2, 1, 1, LANES), lambda i, h: (i, 0, h, 0, 0))
    return pl.pallas_call(
        functools.partial(_mlstm_kernel, n_chunks=nc),
        grid=(b, N_HEADS),
        in_specs=[blk(0), blk(N_HEADS), blk(2 * N_HEADS), blk(3 * N_HEADS),
                  pl.BlockSpec((1, t, LANES), lambda i, h: (i, 0, 0)),
                  pl.BlockSpec((1, nc, LANES, CHUNK), lambda i, h: (i, 0, 0, 0)),
                  st_c, st_v, st_v,
                  pl.BlockSpec((1, 1, HEAD_DIM), lambda i, h: (h, 0, 0))],
        out_specs=[pl.BlockSpec((1, t, HEAD_DIM), lambda i, h: (i, 0, h)), st_c, st_v, st_v],
        out_shape=[jax.ShapeDtypeStruct((b, t, MLSTM_WIDTH), BF16),
                   jax.ShapeDtypeStruct((b, 2, N_HEADS, HEAD_DIM, HEAD_DIM), F32),
                   jax.ShapeDtypeStruct((b, 2, N_HEADS, 1, LANES), F32),
                   jax.ShapeDtypeStruct((b, 2, N_HEADS, 1, LANES), F32)],
        scratch_shapes=[pltpu.VMEM((t, HEAD_DIM), F32), pltpu.VMEM((t, HEAD_DIM), F32),
                        pltpu.VMEM((2, HEAD_DIM, HEAD_DIM), F32), pltpu.VMEM((2, 1, LANES), F32),
                        pltpu.VMEM((2, 1, LANES), F32)],
        compiler_params=_cparams(("arbitrary", "arbitrary"), VMEM_LIMIT),
        name="mlstm",
    )(qkvo3, qkvo3, qkvo3, qkvo3, bc3, bct4, c0, n0, m0, head_g3)


def _dft_cos_sin(n):
    k = np.arange(n)
    ang = 2.0 * np.pi * ((k[:, None] * k[None, :]) % n) / n
    return np.cos(ang), np.sin(ang)


def _channel_dft(scale):
    cd, sd = _dft_cos_sin(GROUP_DIM)
    eye = np.eye(N_GROUPS)
    return (jnp.asarray(np.kron(eye, cd) * scale, F32), jnp.asarray(np.kron(eye, -sd) * scale, F32))


def _fourier_ctx_kernel(u_ref, bdc_ref, bds_ref, ct_ref, st_ref, o_ref):
    u = u_ref[0]
    a = _dot3(u, bdc_ref[...])
    b = _dot3(u, bds_ref[...])
    o_ref[0] = (_dot3(ct_ref[...], a) + _dot3(st_ref[...], b)).astype(BF16)


def _fourier_ctx(u3):
    b, t, w = u3.shape
    bdc, bds = _channel_dft(1.0 / math.sqrt(t * GROUP_DIM))
    ct, st = _dft_cos_sin(t)
    const = lambda i: (0, 0)
    return pl.pallas_call(
        _fourier_ctx_kernel,
        grid=(b,),
        in_specs=[pl.BlockSpec((1, t, w), lambda i: (i, 0, 0)),
                  pl.BlockSpec((w, w), const), pl.BlockSpec((w, w), const),
                  pl.BlockSpec((t, t), const), pl.BlockSpec((t, t), const)],
        out_specs=pl.BlockSpec((1, t, w), lambda i: (i, 0, 0)),
        out_shape=jax.ShapeDtypeStruct((b, t, w), BF16),
        compiler_params=_cparams(("arbitrary",)),
        name="fourier_ctx",
    )(u3, bdc, bds, jnp.asarray(ct, F32), jnp.asarray(st, F32))


def _fourier_chan_kernel(u_ref, bdc_ref, bds_ref, a_ref, b_ref):
    u = u_ref[0]
    a_ref[0] = _dot3(u, bdc_ref[...]).astype(BF16)
    b_ref[0] = _dot3(u, bds_ref[...]).astype(BF16)


def _fourier_pos_kernel(cre_ref, sre_ref, cct_ref, sct_ref, a_ref, b_ref, o_ref, cp_s, sp_s):
    @pl.when(pl.program_id(1) == 0)
    def _():
        gw = cct_ref.shape[0]
        for q in range(cre_ref.shape[0]):
            cr, sr = cre_ref[q], sre_ref[q]
            cc, sc = cct_ref[...], sct_ref[...]
            cp_s[q * gw:(q + 1) * gw, :] = (cr * cc - sr * sc).astype(BF16)
            sp_s[q * gw:(q + 1) * gw, :] = (sr * cc + cr * sc).astype(BF16)

    o_ref[0] = (_dot(cp_s[...], a_ref[0]) + _dot(sp_s[...], b_ref[0])).astype(BF16)


def _fourier_lat(u3, grid_w):
    b, t, w = u3.shape
    rows = t // grid_w
    bdc, bds = _channel_dft(1.0 / math.sqrt(t * GROUP_DIM))
    tm = 512
    const = lambda i, j: (0, 0)
    a, bm = pl.pallas_call(
        _fourier_chan_kernel,
        grid=(b, t // tm),
        in_specs=[pl.BlockSpec((1, tm, w), lambda i, j: (i, j, 0)),
                  pl.BlockSpec((w, w), const), pl.BlockSpec((w, w), const)],
        out_specs=[pl.BlockSpec((1, tm, w), lambda i, j: (i, j, 0))] * 2,
        out_shape=[jax.ShapeDtypeStruct((b, t, w), BF16)] * 2,
        compiler_params=_cparams(("arbitrary", "arbitrary")),
        name="fourier_chan",
    )(u3, bdc, bds)
    cr, sr = _dft_cos_sin(rows)
    cc, sc = _dft_cos_sin(grid_w)
    cre = jnp.asarray(np.repeat(cr, grid_w, axis=1)[:, None, :], F32)
    sre = jnp.asarray(np.repeat(sr, grid_w, axis=1)[:, None, :], F32)
    cct = jnp.asarray(np.tile(cc, (1, rows)), F32)
    sct = jnp.asarray(np.tile(sc, (1, rows)), F32)
    rpt = tm // grid_w
    return pl.pallas_call(
        _fourier_pos_kernel,
        grid=(t // tm, b),
        in_specs=[pl.BlockSpec((rpt, 1, t), lambda j, i: (j, 0, 0)),
                  pl.BlockSpec((rpt, 1, t), lambda j, i: (j, 0, 0)),
                  pl.BlockSpec((grid_w, t), lambda j, i: (0, 0)),
                  pl.BlockSpec((grid_w, t), lambda j, i: (0, 0)),
                  pl.BlockSpec((1, t, w), lambda j, i: (i, 0, 0)),
                  pl.BlockSpec((1, t, w), lambda j, i: (i, 0, 0))],
        out_specs=pl.BlockSpec((1, tm, w), lambda j, i: (i, j, 0)),
        out_shape=jax.ShapeDtypeStruct((b, t, w), BF16),
        scratch_shapes=[pltpu.VMEM((tm, t), BF16), pltpu.VMEM((tm, t), BF16)],
        compiler_params=_cparams(("arbitrary", "arbitrary"), VMEM_LIMIT),
        name="fourier_pos",
    )(cre, sre, cct, sct, a, bm)


def _out_kernel(hm_ref, fu_ref, x_ref, mod_ref, g_ref, wa_ref, wb_ref, rw_ref,
                x1_ref, aug_ref, afft_ref):
    a = _dot(hm_ref[...], wa_ref[...]) + _dot(fu_ref[...], wb_ref[...])
    x1 = x_ref[...] + mod_ref[0, 2:3, :] * a
    x1_ref[...] = x1
    ms = jnp.mean(x1 * x1, axis=-1, keepdims=True)
    y = x1 * lax.rsqrt(ms + EPS) * g_ref[...]
    h2 = y * (1.0 + mod_ref[0, 4:5, :]) + mod_ref[0, 3:4, :]
    logits = _dot3(h2, rw_ref[...])
    lane = lax.broadcasted_iota(I32, logits.shape, 1)
    valid = lane < N_EXPERTS
    lg = jnp.where(valid, logits, -1e30)
    ex = jnp.where(valid, jnp.exp(lg - jnp.max(lg, axis=1, keepdims=True)), 0.0)
    aff = ex / jnp.sum(ex, axis=1, keepdims=True)
    aug_ref[:, :D_MODEL] = h2
    aug_ref[:, D_MODEL:] = aff
    afft_ref[...] = aff.T[:N_EXPERTS, :]


def _out_proj(hm2, fu2, x2, mod, row0, tiles_per_batch, tm, g2, wa, wb, rw):
    n = x2.shape[0]
    const = lambda i: (0, 0)
    return pl.pallas_call(
        _out_kernel,
        grid=(n // tm,),
        in_specs=[pl.BlockSpec((tm, MLSTM_WIDTH), lambda i: (i, 0)),
                  pl.BlockSpec((tm, FOURIER_WIDTH), lambda i: (i, 0)),
                  pl.BlockSpec((tm, D_MODEL), lambda i: (i, 0)),
                  pl.BlockSpec((1, N_MOD, D_MODEL), _mod_index(row0, tiles_per_batch)),
                  pl.BlockSpec((1, D_MODEL), const),
                  pl.BlockSpec((MLSTM_WIDTH, D_MODEL), const),
                  pl.BlockSpec((FOURIER_WIDTH, D_MODEL), const),
                  pl.BlockSpec((D_MODEL, LANES), const)],
        out_specs=[pl.BlockSpec((tm, D_MODEL), lambda i: (i, 0)),
                   pl.BlockSpec((tm, AUG), lambda i: (i, 0)),
                   pl.BlockSpec((N_EXPERTS, tm), lambda i: (0, i))],
        out_shape=[jax.ShapeDtypeStruct((n, D_MODEL), F32),
                   jax.ShapeDtypeStruct((n, AUG), F32),
                   jax.ShapeDtypeStruct((N_EXPERTS, n), F32)],
        compiler_params=_cparams(("arbitrary",), VMEM_LIMIT),
        name="out_proj",
    )(hm2, fu2, x2, mod, g2, wa, wb, rw)


def _topk_kernel(aff_ref, pos_ref, blk_ref, *, n, cap):
    nb = n // LANES
    aff = aff_ref[...]

    def enough(t):
        return jnp.sum(jnp.where(aff >= t, 1.0, 0.0), axis=1, keepdims=True) >= cap

    def pow2(k):
        return lax.bitcast_convert_type(jnp.left_shift(127 - k, 23), F32)

    def exp_search(_, c):
        lo, hi = c
        mid = jnp.right_shift(lo + hi, 1)
        ok = enough(pow2(jnp.minimum(mid, 126)))
        return jnp.where(ok, lo, mid + 1), jnp.where(ok, mid, hi)

    kz = jnp.zeros((N_EXPERTS, 1), I32)
    kstar, _ = lax.fori_loop(0, 7, exp_search, (kz, kz + 127))
    found = kstar < 127
    p = pow2(jnp.minimum(kstar, 126))
    t_lo0 = jnp.where(found, p, 0.0)
    t_hi0 = jnp.where(found, 2.0 * p, p)

    def bisect(_, c):
        t_lo, t_hi = c
        mid = 0.5 * (t_lo + t_hi)
        ok = enough(mid)
        return jnp.where(ok, mid, t_lo), jnp.where(ok, t_hi, mid)

    t_lo, t_hi = lax.fori_loop(0, 40, bisect, (t_lo0, t_hi0))
    n_gt = jnp.sum(jnp.where(aff >= t_hi, 1.0, 0.0), axis=1, keepdims=True)
    need = cap - n_gt
    row = lax.broadcasted_iota(I32, (LANES, LANES), 0)
    col = lax.broadcasted_iota(I32, (LANES, LANES), 1)
    upper = (row <= col).astype(BF16)
    blane = lax.broadcasted_iota(I32, blk_ref.shape, 1)

    def block(b, carry):
        eq_off, sel_off = carry
        sl = pl.ds(pl.multiple_of(b * LANES, LANES), LANES)
        ab = aff_ref[:, sl]
        gt = ab >= t_hi
        eq = (ab >= t_lo) & (ab < t_hi)
        eqf = jnp.where(eq, 1.0, 0.0)
        eq_rank = _dot(eqf.astype(BF16), upper) + eq_off - eqf
        sel = gt | (eq & (eq_rank < need))
        self_ = jnp.where(sel, 1.0, 0.0)
        cum = _dot(self_.astype(BF16), upper) + sel_off
        pos_ref[:, sl] = jnp.where(sel, cum - 1.0, -1.0).astype(I32)
        blk_ref[...] = jnp.where(blane == b, sel_off.astype(I32), blk_ref[...])
        return (eq_off + jnp.sum(eqf, axis=1, keepdims=True),
                sel_off + jnp.sum(self_, axis=1, keepdims=True))

    blk_ref[...] = jnp.full(blk_ref.shape, cap, I32)
    zero = jnp.zeros((N_EXPERTS, 1), F32)
    lax.fori_loop(0, nb, block, (zero, zero))


def _topk(aff_t, cap):
    n = aff_t.shape[1]
    return pl.pallas_call(
        functools.partial(_topk_kernel, n=n, cap=cap),
        grid=(1,),
        in_specs=[pl.BlockSpec((N_EXPERTS, n), lambda i: (0, 0))],
        out_specs=[pl.BlockSpec((N_EXPERTS, n), lambda i: (0, 0)),
                   pl.BlockSpec((N_EXPERTS, 2 * LANES), lambda i: (0, 0))],
        out_shape=[jax.ShapeDtypeStruct((N_EXPERTS, n), I32),
                   jax.ShapeDtypeStruct((N_EXPERTS, 2 * LANES), I32)],
        compiler_params=_cparams(("arbitrary",)),
        name="topk",
    )(aff_t)


def _compact_kernel(blk_sm, pos_ref, out_ref, *, nb):
    e = pl.program_id(0)

    slot0 = lax.broadcasted_iota(I32, (LANES, LANES), 0)
    tok = lax.broadcasted_iota(I32, (LANES, LANES), 1)

    def chunk(jc, carry):
        first, last = carry
        j0 = jc * LANES
        first = lax.while_loop(lambda b: blk_sm[e, b + 1] <= j0, lambda b: b + 1, first)
        last = lax.while_loop(lambda b: (b < nb) & (blk_sm[e, b] < j0 + LANES), lambda b: b + 1, last)
        slot = slot0 + j0

        def body(b, acc):
            prow = pos_ref[0, :, pl.ds(pl.multiple_of(b * LANES, LANES), LANES)]
            return acc + jnp.where(prow == slot, (tok + b * LANES).astype(F32), 0.0)

        acc = lax.fori_loop(first, last, body, jnp.zeros((LANES, LANES), F32))
        out_ref[0, jc] = jnp.sum(acc.T, axis=0, keepdims=True).astype(I32)
        return first, last

    lax.fori_loop(0, out_ref.shape[1], chunk, (jnp.int32(0), jnp.int32(0)))


def _compact(pos, blk, cap):
    n = pos.shape[1]
    ncj = cap // LANES
    out = pl.pallas_call(
        functools.partial(_compact_kernel, nb=n // LANES),
        grid_spec=pltpu.PrefetchScalarGridSpec(
            num_scalar_prefetch=1,
            grid=(N_EXPERTS,),
            in_specs=[pl.BlockSpec((1, 1, n), lambda e, blk: (e, 0, 0))],
            out_specs=pl.BlockSpec((1, ncj, 1, LANES), lambda e, blk: (e, 0, 0, 0)),
        ),
        out_shape=jax.ShapeDtypeStruct((N_EXPERTS, ncj, 1, LANES), I32),
        compiler_params=_cparams(("arbitrary",)),
        name="compact",
    )(blk, pos.reshape(N_EXPERTS, 1, n))
    return out.reshape(N_EXPERTS, cap)


def _moe_kernel(lists_sm, aug_c, aug_l, w1_ref, w3_ref, w2_ref, ye_ref,
                xbuf, xe, gate, act_all, w2_all, w1b, w3b, sem, *, cap_c, cap_l, row_chunk, unroll, tf):
    e = pl.program_id(0)
    f = pl.program_id(1)
    ne = pl.num_programs(0)
    nf = pl.num_programs(1)
    rows = cap_c + cap_l

    def gather_rows(expert):
        base = expert * rows

        def body_for(aug):
            def body(g, carry):
                for u in range(unroll):
                    j = g * unroll + u
                    pltpu.make_async_copy(aug.at[pl.ds(lists_sm[base + j], 1)], xbuf.at[pl.ds(j, 1)],
                                          sem.at[0]).start()
                return carry
            return body

        lax.fori_loop(0, cap_c // unroll, body_for(aug_c), 0)
        lax.fori_loop(cap_c // unroll, rows // unroll, body_for(aug_l), 0)

    @pl.when((e == 0) & (f == 0))
    def _():
        gather_rows(e)

    @pl.when(f == 0)
    def _():
        pltpu.make_async_copy(xbuf, xbuf, sem.at[0]).wait()
        xe[...] = xbuf[:, :D_MODEL].astype(BF16)
        lane = lax.broadcasted_iota(I32, (rows, LANES), 1)
        gate[...] = jnp.sum(jnp.where(lane == e, xbuf[:, D_MODEL:], 0.0), axis=1, keepdims=True)
        ye_ref[0, rows:, :] = jnp.zeros((ye_ref.shape[1] - rows, D_MODEL), BF16)

        @pl.when(e + 1 < ne)
        def _():
            gather_rows(e + 1)

    w1b[...] = w1_ref[0].astype(BF16)
    w3b[...] = w3_ref[0].astype(BF16)
    fsl = pl.ds(pl.multiple_of(f * tf, tf), tf)
    w2_all[fsl, :] = w2_ref[0].astype(BF16)
    for r in range(rows // row_chunk):
        sl = pl.ds(r * row_chunk, row_chunk)
        x = xe[sl, :]
        a = _dot(x, w1b[...])
        bg = _dot(x, w3b[...])
        act_all[sl, fsl] = (a * jax.nn.sigmoid(a) * bg).astype(BF16)

    @pl.when(f == nf - 1)
    def _():
        for r in range(rows // row_chunk):
            sl = pl.ds(r * row_chunk, row_chunk)
            ye_ref[0, sl, :] = (_dot(act_all[sl, :], w2_all[...]) * gate[sl, :]).astype(BF16)


def _moe(lists, aug_c, aug_l, w1, w3, w2, cap_c, cap_l, tf=256, row_chunk=512, unroll=8):
    rows = cap_c + cap_l
    any_spec = pl.BlockSpec(memory_space=pl.ANY)
    return pl.pallas_call(
        functools.partial(_moe_kernel, cap_c=cap_c, cap_l=cap_l, row_chunk=row_chunk, unroll=unroll,
                          tf=tf),
        grid_spec=pltpu.PrefetchScalarGridSpec(
            num_scalar_prefetch=1,
            grid=(N_EXPERTS, D_EXPERT // tf),
            in_specs=[any_spec, any_spec,
                      pl.BlockSpec((1, D_MODEL, tf), lambda e, f, ls: (e, 0, f)),
                      pl.BlockSpec((1, D_MODEL, tf), lambda e, f, ls: (e, 0, f)),
                      pl.BlockSpec((1, tf, D_MODEL), lambda e, f, ls: (e, f, 0))],
            out_specs=pl.BlockSpec((1, rows + WINDOW, D_MODEL), lambda e, f, ls: (e, 0, 0)),
            scratch_shapes=[pltpu.VMEM((rows, AUG), F32), pltpu.VMEM((rows, D_MODEL), BF16),
                            pltpu.VMEM((rows, 1), F32),
                            pltpu.VMEM((rows, D_EXPERT), BF16), pltpu.VMEM((D_EXPERT, D_MODEL), BF16),
                            pltpu.VMEM((D_MODEL, tf), BF16), pltpu.VMEM((D_MODEL, tf), BF16),
                            pltpu.SemaphoreType.DMA((1,))],
        ),
        out_shape=jax.ShapeDtypeStruct((N_EXPERTS, rows + WINDOW, D_MODEL), BF16),
        compiler_params=_cparams(("arbitrary", "arbitrary"), VMEM_LIMIT),
        name="moe",
    )(lists, aug_c, aug_l, w1, w3, w2)


def _combine_kernel(seg_sm, x1_ref, pos_ref, mod_ref, g_ref, ye_hbm, o_ref, wbuf, sem, *, base, tm):
    i = pl.program_id(0)
    nt = pl.num_programs(0)
    bpt = tm // LANES
    slot = lax.rem(i, 2)
    last_start = ye_hbm.shape[1] - WINDOW
    tn = (((0,), (0,)), ((), ()))

    def start_windows(tile, rnd, buf):
        for e in range(N_EXPERTS):
            start = jnp.minimum(base + (seg_sm[e, tile * bpt] & -ROW_ALIGN) + rnd * WINDOW, last_start)
            pltpu.make_async_copy(ye_hbm.at[e, pl.ds(pl.multiple_of(start, ROW_ALIGN), WINDOW)],
                                  wbuf.at[buf, pl.ds(e * WINDOW, WINDOW)], sem.at[buf]).start()

    def wait_windows(buf):
        pltpu.make_async_copy(wbuf.at[buf], wbuf.at[buf], sem.at[buf]).wait()

    @pl.when(i == 0)
    def _():
        start_windows(i, 0, slot)

    @pl.when(i + 1 < nt)
    def _():
        start_windows(i + 1, 0, 1 - slot)

    most = jnp.int32(0)
    for e in range(N_EXPERTS):
        most = jnp.maximum(most, seg_sm[e, (i + 1) * bpt] - (seg_sm[e, i * bpt] & -ROW_ALIGN))
    n_rounds = jnp.right_shift(most + (WINDOW - 1), WINDOW_SHIFT)

    pos = pos_ref[...]
    valid = pos >= 0
    first = jnp.min(jnp.where(valid, pos, jnp.int32(2 ** 30)), axis=1, keepdims=True)
    rel = pos - (first & -ROW_ALIGN)
    lane = lax.broadcasted_iota(I32, (N_EXPERTS, N_EXPERTS * WINDOW), 1)
    owner = lax.broadcasted_iota(I32, (N_EXPERTS, N_EXPERTS * WINDOW), 0)
    spread = jnp.where(jnp.right_shift(lane, WINDOW_SHIFT) == owner, 1.0, 0.0).astype(BF16)
    wrow = (lax.broadcasted_iota(I32, (1, N_EXPERTS * WINDOW), 1) & (WINDOW - 1)).astype(F32)

    def place(rnd, y):
        q = jnp.where(valid & (rel >= rnd * WINDOW) & (rel < (rnd + 1) * WINDOW), rel - rnd * WINDOW, -1)
        qb = lax.dot_general(q.astype(F32).astype(BF16), spread, tn, preferred_element_type=F32)
        onehot = jnp.where(qb == wrow, 1.0, 0.0).astype(BF16)
        return y + _dot(onehot, wbuf[slot])

    wait_windows(slot)
    y = place(0, jnp.zeros((tm, D_MODEL), F32))

    def extra_round(rnd, y):
        start_windows(i, rnd, slot)
        wait_windows(slot)
        return place(rnd, y)

    y = lax.fori_loop(1, n_rounds, extra_round, y)
    x = x1_ref[...] + mod_ref[0, 5:6, :] * y
    ms = jnp.mean(x * x, axis=-1, keepdims=True)
    o_ref[...] = x * lax.rsqrt(ms + EPS) * g_ref[...]


def _combine_norm(seg, x1, pos, ye, base, mod, row0, tiles_per_batch, tm, gf):
    n = x1.shape[0]
    return pl.pallas_call(
        functools.partial(_combine_kernel, base=base, tm=tm),
        grid_spec=pltpu.PrefetchScalarGridSpec(
            num_scalar_prefetch=1,
            grid=(n // tm,),
            in_specs=[pl.BlockSpec((tm, D_MODEL), lambda i, sg: (i, 0)),
                      pl.BlockSpec((N_EXPERTS, tm), lambda i, sg: (0, i)),
                      pl.BlockSpec((1, N_MOD, D_MODEL),
                                   lambda i, sg: (row0 + i // tiles_per_batch, 0, 0)),
                      pl.BlockSpec((1, D_MODEL), lambda i, sg: (0, 0)),
                      pl.BlockSpec(memory_space=pl.ANY)],
            out_specs=pl.BlockSpec((tm, D_MODEL), lambda i, sg: (i, 0)),
            scratch_shapes=[pltpu.VMEM((2, N_EXPERTS * WINDOW, D_MODEL), BF16),
                            pltpu.SemaphoreType.DMA((2,))],
        ),
        out_shape=jax.ShapeDtypeStruct((n, D_MODEL), F32),
        compiler_params=_cparams(("arbitrary",), VMEM_LIMIT),
        name="combine_norm",
    )(seg, x1, pos, mod, gf, ye)


def kernel(x_prompt, x_sample, state_C, state_n, state_m, c, c_ctx, ada_w, ada_b, norm1_g, norm2_g,
           w_in, b_in, head_g, w_out, router_w, exp_w1, exp_w3, exp_w2, final_g):
    depth = ada_w.shape[0]
    assert depth == 1, "single-layer trunk"
    bc_, tc_, _ = x_prompt.shape
    bl_, tl_, _ = x_sample.shape
    w4 = 4 * MLSTM_WIDTH
    tm = 512

    cvec = jnp.concatenate([c_ctx[None, :], c, jnp.zeros((8 - 1 - bl_, D_MODEL), F32)], axis=0)
    mod = _modulation(cvec, ada_w[0], ada_b[0])

    wi = w_in[0]
    bi = b_in[0]
    wq = wi[:, :w4].astype(BF16)
    bq = bi[None, :w4]
    wg = jnp.pad(wi[:, w4:w4 + 16], ((0, 0), (0, LANES - 16))).astype(BF16)
    bg = jnp.pad(bi[w4:w4 + 16], (0, LANES - 16))[None, :]
    wf = wi[:, w4 + 16:].astype(BF16)
    bfo = bi[None, w4 + 16:]
    colscale = jnp.ones((1, w4), F32).at[:, MLSTM_WIDTH:2 * MLSTM_WIDTH].set(HEAD_DIM ** -0.5)
    g1 = norm1_g[0][None, :]
    g2 = norm2_g[0][None, :]
    wo = w_out[0].astype(BF16)
    wa, wb = wo[:MLSTM_WIDTH], wo[MLSTM_WIDTH:]
    rw = jnp.pad(router_w[0], ((0, 0), (0, LANES - N_EXPERTS)))
    hg3 = head_g[0][:, None, :]

    def spread(v):
        return jnp.broadcast_to(v[..., None, None], v.shape + (1, LANES))

    def mixer_and_router(x3, row0, init, grid_w):
        b, t, _ = x3.shape
        x2 = x3.reshape(b * t, D_MODEL)
        tpb = max(t // tm, 1) if row0 else (b * t) // tm + 1
        qkvo, gates, u = _in_proj(x2, mod, row0, tpb, tm, g1, wq, bq, colscale, wg, bg, wf, bfo)
        bcs, bct = _gate_sums(gates)
        c0, n0, m0 = init
        hm, cf, nf, mf = _mlstm(qkvo.reshape(b, t, w4), bcs.reshape(b, t, LANES),
                                bct.reshape(b, t // CHUNK, LANES, CHUNK), c0, n0, m0, hg3)
        u3 = u.reshape(b, t, FOURIER_WIDTH)
        fu = _fourier_ctx(u3) if grid_w is None else _fourier_lat(u3, grid_w)
        x1, aug, aff_t = _out_proj(hm.reshape(b * t, MLSTM_WIDTH), fu.reshape(b * t, FOURIER_WIDTH),
                                   x2, mod, row0, tpb, tm, g2, wa, wb, rw)
        cap = CAPACITY_FACTOR * (b * t) // N_EXPERTS
        pos, blk = _topk(aff_t, cap)
        return x1, aug, _compact(pos, blk, cap), cap, (cf, nf, mf), (pos, blk)

    zero_init = (jnp.zeros((bc_, 2, N_HEADS, HEAD_DIM, HEAD_DIM), F32),
                 jnp.zeros((bc_, 2, N_HEADS, 1, LANES), F32),
                 jnp.zeros((bc_, 2, N_HEADS, 1, LANES), F32))
    lat_init = (state_C[:, 0], state_n[:, 0][..., None, :], spread(state_m[:, 0]))
    x1c, augc, listc, capc, (cf, nf, mf), (posc, blkc) = mixer_and_router(x_prompt, 0, zero_init, None)
    x1l, augl, listl, capl, _, (posl, blkl) = mixer_and_router(x_sample, 1, lat_init, GRID_W)

    lists = jnp.concatenate([listc, listl], axis=1).reshape(-1)
    ye = _moe(lists, augc, augl, exp_w1[0], exp_w3[0], exp_w2[0], capc, capl)
    gf = final_g[None, :]
    tc = 256
    y_prompt = _combine_norm(blkc, x1c, posc, ye, 0, mod, 0, bc_ * tc_ // tc + 1, tc, gf)
    y_sample = _combine_norm(blkl, x1l, posl, ye, capc, mod, 1, tl_ // tc, tc, gf)
    y_prompt = y_prompt.reshape(x_prompt.shape)
    y_sample = y_sample.reshape(x_sample.shape)
    new_c = cf[:, None]
    new_n = nf[:, None, :, :, 0, :]
    new_m = mf[:, None, :, :, 0, 0]
    return (y_prompt, y_sample, new_c, new_n, new_m)
```

```python
import functools
import math

import numpy as np
import jax
import jax.numpy as jnp
from jax import lax
from jax.experimental import pallas as pl
from jax.experimental.pallas import tpu as pltpu

F32 = jnp.float32
BF16 = jnp.bfloat16
I32 = jnp.int32

D_MODEL = 1024
N_HEADS = 4
HEAD_DIM = 128
MLSTM_WIDTH = N_HEADS * HEAD_DIM
FOURIER_WIDTH = 512
N_GROUPS = 4
GROUP_DIM = 128
CHUNK = 128
GRID_W = 64
N_EXPERTS = 16
CAPACITY_FACTOR = 2
D_EXPERT = 2048
N_MOD = 6
EPS = 1e-6
LANES = 128
GATE_ROWS = 8
STATE_PAD = 8
ROW_ALIGN = 16
AUG = D_MODEL + LANES
WINDOW_SHIFT = 6
WINDOW = 1 << WINDOW_SHIFT
VMEM_LIMIT = 56 * 1024 * 1024


def _cparams(sem, vmem=None):
    return pltpu.CompilerParams(dimension_semantics=sem, vmem_limit_bytes=vmem)


def _split3(a):
    a1 = a.astype(BF16)
    r1 = a - a1.astype(F32)
    a2 = r1.astype(BF16)
    a3 = (r1 - a2.astype(F32)).astype(BF16)
    return a1, a2, a3


def _dot(a, b):
    return jnp.dot(a, b, preferred_element_type=F32)


def _dot3(a, b):
    a1, a2, _ = _split3(a)
    b1, b2, _ = _split3(b)
    return _dot(a1, b1) + (_dot(a1, b2) + _dot(a2, b1))


def _mod_kernel(c_ref, w_ref, b_ref, o_ref):
    c = c_ref[...]
    s = c * jax.nn.sigmoid(c)
    o_ref[...] = _dot(s.astype(BF16), w_ref[...].astype(BF16)) + b_ref[...]


def _modulation(cvec8, ada_w, ada_b):
    tn = D_MODEL
    out = pl.pallas_call(
        _mod_kernel,
        grid=(N_MOD,),
        in_specs=[pl.BlockSpec((8, D_MODEL), lambda j: (0, 0)),
                  pl.BlockSpec((D_MODEL, tn), lambda j: (0, j)),
                  pl.BlockSpec((1, tn), lambda j: (0, j))],
        out_specs=pl.BlockSpec((8, tn), lambda j: (0, j)),
        out_shape=jax.ShapeDtypeStruct((8, N_MOD * D_MODEL), F32),
        compiler_params=_cparams(("arbitrary",)),
        name="modulation",
    )(cvec8, ada_w, ada_b.reshape(1, -1))
    return out.reshape(8, N_MOD, D_MODEL)


def _mod_index(row0, tiles_per_batch):
    return lambda i: (row0 + i // tiles_per_batch, 0, 0)


def _in_kernel(x_ref, mod_ref, g_ref, wq_ref, bq_ref, sc_ref, wg_ref, bg_ref, wf_ref, bf_ref,
               qkvo_ref, gates_ref, u_ref):
    x = x_ref[...]
    ms = jnp.mean(x * x, axis=-1, keepdims=True)
    y = x * lax.rsqrt(ms + EPS) * g_ref[...]
    h = y * (1.0 + mod_ref[0, 1:2, :]) + mod_ref[0, 0:1, :]
    hb = h.astype(BF16)
    p = (_dot(hb, wq_ref[...]) + bq_ref[...]) * sc_ref[...]
    qkvo_ref[...] = p.astype(BF16)
    gates_ref[...] = _dot(hb, wg_ref[...]) + bg_ref[...]
    u_ref[...] = _dot(hb, wf_ref[...]) + bf_ref[...]


def _in_proj(x2, mod, row0, tiles_per_batch, tm, g1, wq, bq, sc, wg, bg, wf, bfo):
    n = x2.shape[0]
    wq_n = wq.shape[1]
    const = lambda i: (0, 0)
    return pl.pallas_call(
        _in_kernel,
        grid=(n // tm,),
        in_specs=[pl.BlockSpec((tm, D_MODEL), lambda i: (i, 0)),
                  pl.BlockSpec((1, N_MOD, D_MODEL), _mod_index(row0, tiles_per_batch)),
                  pl.BlockSpec((1, D_MODEL), const),
                  pl.BlockSpec((D_MODEL, wq_n), const), pl.BlockSpec((1, wq_n), const),
                  pl.BlockSpec((1, wq_n), const),
                  pl.BlockSpec((D_MODEL, LANES), const), pl.BlockSpec((1, LANES), const),
                  pl.BlockSpec((D_MODEL, FOURIER_WIDTH), const), pl.BlockSpec((1, FOURIER_WIDTH), const)],
        out_specs=[pl.BlockSpec((tm, wq_n), lambda i: (i, 0)),
                   pl.BlockSpec((tm, LANES), lambda i: (i, 0)),
                   pl.BlockSpec((tm, FOURIER_WIDTH), lambda i: (i, 0))],
        out_shape=[jax.ShapeDtypeStruct((n, wq_n), BF16),
                   jax.ShapeDtypeStruct((n, LANES), F32),
                   jax.ShapeDtypeStruct((n, FOURIER_WIDTH), F32)],
        compiler_params=_cparams(("arbitrary",), VMEM_LIMIT),
        name="in_proj",
    )(x2, mod, g1, wq, bq, sc, wg, bg, wf, bfo)


def _gates_kernel(g_ref, out_ref):
    row = lax.broadcasted_iota(I32, (CHUNK, CHUNK), 0)
    col = lax.broadcasted_iota(I32, (CHUNK, CHUNK), 1)
    tril = (col <= row).astype(BF16)
    triu = (col >= row).astype(BF16)
    lane = lax.broadcasted_iota(I32, (CHUNK, LANES), 1)
    out_ref[...] = jnp.zeros(out_ref.shape, F32)
    for c in range(out_ref.shape[0]):
        g = g_ref[pl.ds(c * CHUNK, CHUNK), :]
        ls = jnp.minimum(g, 0.0) - jnp.log(1.0 + jnp.exp(-jnp.abs(g)))
        l1, l2, l3 = _split3(ls)
        pre = _dot(tril, l1) + _dot(tril, l2) + _dot(tril, l3)
        suf = _dot(triu, l1) + _dot(triu, l2) + _dot(triu, l3)
        bct = jnp.where((lane >= 4) & (lane < 8), pre,
                        jnp.where((lane >= 12) & (lane < 16), suf, g)).T
        for head in range(N_HEADS):
            for d in range(2):
                ich = 2 * N_HEADS * d + head
                fch = ich + N_HEADS
                out_ref[c, 2 * head + d, 0:1, :] = bct[ich:ich + 1, :]
                out_ref[c, 2 * head + d, 1:2, :] = bct[fch:fch + 1, :]


def _gate_sums(gates, chunks_per_step=8):
    n = gates.shape[0]
    nc = n // CHUNK
    cps = chunks_per_step
    return pl.pallas_call(
        _gates_kernel,
        grid=(nc // cps,),
        in_specs=[pl.BlockSpec((cps * CHUNK, LANES), lambda i: (i, 0))],
        out_specs=pl.BlockSpec((cps, 2 * N_HEADS, GATE_ROWS, CHUNK), lambda i: (i, 0, 0, 0)),
        out_shape=jax.ShapeDtypeStruct((nc, 2 * N_HEADS, GATE_ROWS, CHUNK), F32),
        compiler_params=_cparams(("arbitrary",)),
        name="gate_sums",
    )(gates)


def _mlstm_kernel(qkvo_ref, g_ref, c0_ref, n0_ref, m0_ref, hg_ref,
                  hm_ref, cf_ref, nf_ref, mf_ref, hf_s, hb_s, c_s, m_s, *, n_chunks):
    c_s[:, :, :HEAD_DIM, :] = c0_ref[0]
    c_s[:, :, HEAD_DIM:, :] = jnp.broadcast_to(n0_ref[0], (2, N_HEADS, STATE_PAD, HEAD_DIM))
    m_s[...] = m0_ref[0]
    src = lax.broadcasted_iota(I32, (CHUNK, CHUNK), 0)
    dst = lax.broadcasted_iota(I32, (CHUNK, CHUNK), 1)
    nt = (((1,), (1,)), ((), ()))

    def one_direction(c, head, d, c_prev, m_prev):
        sl = pl.ds(pl.multiple_of(c * CHUNK, CHUNK), CHUNK)
        col = lambda part: pl.ds((part * N_HEADS + head) * HEAD_DIM, HEAD_DIM)
        q = qkvo_ref[0, sl, col(0)]
        k = qkvo_ref[0, sl, col(1)]
        vt = qkvo_ref[0, sl, col(2)].astype(F32).T
        gates = g_ref[0, c, 2 * head + d]
        ig_row, b_row = gates[0:1, :], gates[1:2, :]
        gates_t = gates.T
        ig_col, b_col = gates_t[:, 0:1], gates_t[:, 1:2]
        if d == 0:
            g = b_row[:, CHUNK - 1:CHUNK]
            mask = src <= dst
        else:
            g = b_row[:, 0:1]
            mask = src >= dst
        w_log = g - b_row + ig_row
        m_loc = jnp.max(w_log, axis=1, keepdims=True)
        w = jnp.exp(w_log - m_loc)
        lhs = jnp.concatenate([vt * w, jnp.broadcast_to(w, (STATE_PAD, CHUNK))], axis=0)
        c_loc = _dot(lhs.astype(BF16), k)
        dmat = jnp.where(mask, b_row - b_col + ig_col, -jnp.inf)
        inter_log = b_row + m_prev
        m_comb = jnp.maximum(inter_log, jnp.max(dmat, axis=0, keepdims=True))
        s_t = lax.dot_general(k, q, nt, preferred_element_type=F32) * jnp.exp(dmat - m_comb)
        w_inter = jnp.exp(inter_log - m_comb)
        cq = lax.dot_general(c_prev.astype(BF16), q, nt, preferred_element_type=F32)
        num = _dot(vt.astype(BF16), s_t.astype(BF16)) + w_inter * cq[:HEAD_DIM, :]
        den = jnp.sum(s_t, axis=0, keepdims=True) + w_inter * cq[HEAD_DIM:HEAD_DIM + 1, :]
        h = num / jnp.maximum(jnp.abs(den), jnp.exp(-m_comb))
        m_new = jnp.maximum(g + m_prev, m_loc)
        a = jnp.exp(g + m_prev - m_new)
        bb = jnp.exp(m_loc - m_new)
        return h, a * c_prev + bb * c_loc, m_new

    def scan_body(i, carry):
        units = [(head, d) for head in range(N_HEADS) for d in range(2)]
        prev = [(c_s[d, head], m_s[d, head][:, 0:1]) for head, d in units]
        chunk = [i, n_chunks - 1 - i]
        new = [one_direction(chunk[d], head, d, *prev[u]) for u, (head, d) in enumerate(units)]
        for (head, d), (h, c_new, m_new) in zip(units, new):
            sl = pl.ds(pl.multiple_of(chunk[d] * CHUNK, CHUNK), CHUNK)
            (hf_s, hb_s)[d][head, :, sl] = h
            c_s[d, head] = c_new
            m_s[d, head] = jnp.broadcast_to(m_new, (1, LANES))
        return carry

    lax.fori_loop(0, n_chunks, scan_body, 0)

    def out_body(c, carry):
        sl = pl.ds(pl.multiple_of(c * CHUNK, CHUNK), CHUNK)
        for head in range(N_HEADS):
            hs = hf_s[head, :, sl] + hb_s[head, :, sl]
            hn = (hs * lax.rsqrt(jnp.mean(hs * hs, axis=0, keepdims=True) + EPS)).T * hg_ref[head]
            o = qkvo_ref[0, sl, pl.ds((3 * N_HEADS + head) * HEAD_DIM, HEAD_DIM)].astype(F32)
            hm_ref[0, sl, pl.ds(head * HEAD_DIM, HEAD_DIM)] = (hn * jax.nn.sigmoid(o)).astype(BF16)
        return carry

    lax.fori_loop(0, n_chunks, out_body, 0)
    cf_ref[0] = c_s[:, :, :HEAD_DIM, :]
    nf_ref[0] = c_s[:, :, HEAD_DIM:HEAD_DIM + 1, :]
    mf_ref[0] = m_s[...]


def _mlstm(qkvo3, gates5, c0, n0, m0, head_g3):
    b, t, w4 = qkvo3.shape
    nc = t // CHUNK
    once = dict(pipeline_mode=pl.Buffered(1))
    st_c = pl.BlockSpec((1, 2, N_HEADS, HEAD_DIM, HEAD_DIM), lambda i: (i, 0, 0, 0, 0))
    st_v = pl.BlockSpec((1, 2, N_HEADS, 1, LANES), lambda i: (i, 0, 0, 0, 0))
    return pl.pallas_call(
        functools.partial(_mlstm_kernel, n_chunks=nc),
        grid=(b,),
        in_specs=[pl.BlockSpec((1, t, w4), lambda i: (i, 0, 0), **once),
                  pl.BlockSpec((1, nc, 2 * N_HEADS, GATE_ROWS, CHUNK), lambda i: (i, 0, 0, 0, 0), **once),
                  st_c, st_v, st_v,
                  pl.BlockSpec((N_HEADS, 1, HEAD_DIM), lambda i: (0, 0, 0))],
        out_specs=[pl.BlockSpec((1, t, MLSTM_WIDTH), lambda i: (i, 0, 0), **once), st_c, st_v, st_v],
        out_shape=[jax.ShapeDtypeStruct((b, t, MLSTM_WIDTH), BF16),
                   jax.ShapeDtypeStruct((b, 2, N_HEADS, HEAD_DIM, HEAD_DIM), F32),
                   jax.ShapeDtypeStruct((b, 2, N_HEADS, 1, LANES), F32),
                   jax.ShapeDtypeStruct((b, 2, N_HEADS, 1, LANES), F32)],
        scratch_shapes=[pltpu.VMEM((N_HEADS, HEAD_DIM, t), F32), pltpu.VMEM((N_HEADS, HEAD_DIM, t), F32),
                        pltpu.VMEM((2, N_HEADS, HEAD_DIM + STATE_PAD, HEAD_DIM), F32),
                        pltpu.VMEM((2, N_HEADS, 1, LANES), F32)],
        compiler_params=_cparams(("arbitrary",), VMEM_LIMIT),
        name="mlstm",
    )(qkvo3, gates5, c0, n0, m0, head_g3)


def _dft_cos_sin(n):
    k = np.arange(n)
    ang = 2.0 * np.pi * ((k[:, None] * k[None, :]) % n) / n
    return np.cos(ang), np.sin(ang)


def _channel_dft(scale):
    cd, sd = _dft_cos_sin(GROUP_DIM)
    eye = np.eye(N_GROUPS)
    return (jnp.asarray(np.kron(eye, cd) * scale, F32), jnp.asarray(np.kron(eye, -sd) * scale, F32))


def _fourier_ctx_kernel(u_ref, bdc_ref, bds_ref, ct_ref, st_ref, o_ref):
    u = u_ref[0]
    a = _dot3(u, bdc_ref[...])
    b = _dot3(u, bds_ref[...])
    o_ref[0] = (_dot3(ct_ref[...], a) + _dot3(st_ref[...], b)).astype(BF16)


def _fourier_ctx(u3):
    b, t, w = u3.shape
    bdc, bds = _channel_dft(1.0 / math.sqrt(t * GROUP_DIM))
    ct, st = _dft_cos_sin(t)
    const = lambda i: (0, 0)
    return pl.pallas_call(
        _fourier_ctx_kernel,
        grid=(b,),
        in_specs=[pl.BlockSpec((1, t, w), lambda i: (i, 0, 0)),
                  pl.BlockSpec((w, w), const), pl.BlockSpec((w, w), const),
                  pl.BlockSpec((t, t), const), pl.BlockSpec((t, t), const)],
        out_specs=pl.BlockSpec((1, t, w), lambda i: (i, 0, 0)),
        out_shape=jax.ShapeDtypeStruct((b, t, w), BF16),
        compiler_params=_cparams(("arbitrary",)),
        name="fourier_ctx",
    )(u3, bdc, bds, jnp.asarray(ct, F32), jnp.asarray(st, F32))


def _fourier_chan_kernel(u_ref, bdc_ref, bds_ref, a_ref, b_ref):
    u = u_ref[0]
    a_ref[0] = _dot3(u, bdc_ref[...]).astype(BF16)
    b_ref[0] = _dot3(u, bds_ref[...]).astype(BF16)


def _fourier_pos_kernel(cre_ref, sre_ref, cct_ref, sct_ref, a_ref, b_ref, o_ref, cp_s, sp_s):
    @pl.when(pl.program_id(1) == 0)
    def _():
        gw = cct_ref.shape[0]
        for q in range(cre_ref.shape[0]):
            cr, sr = cre_ref[q], sre_ref[q]
            cc, sc = cct_ref[...], sct_ref[...]
            cp_s[q * gw:(q + 1) * gw, :] = (cr * cc - sr * sc).astype(BF16)
            sp_s[q * gw:(q + 1) * gw, :] = (sr * cc + cr * sc).astype(BF16)

    o_ref[0] = (_dot(cp_s[...], a_ref[0]) + _dot(sp_s[...], b_ref[0])).astype(BF16)


def _fourier_lat(u3, grid_w):
    b, t, w = u3.shape
    rows = t // grid_w
    bdc, bds = _channel_dft(1.0 / math.sqrt(t * GROUP_DIM))
    tm = 512
    const = lambda i, j: (0, 0)
    a, bm = pl.pallas_call(
        _fourier_chan_kernel,
        grid=(b, t // tm),
        in_specs=[pl.BlockSpec((1, tm, w), lambda i, j: (i, j, 0)),
                  pl.BlockSpec((w, w), const), pl.BlockSpec((w, w), const)],
        out_specs=[pl.BlockSpec((1, tm, w), lambda i, j: (i, j, 0))] * 2,
        out_shape=[jax.ShapeDtypeStruct((b, t, w), BF16)] * 2,
        compiler_params=_cparams(("arbitrary", "arbitrary")),
        name="fourier_chan",
    )(u3, bdc, bds)
    cr, sr = _dft_cos_sin(rows)
    cc, sc = _dft_cos_sin(grid_w)
    cre = jnp.asarray(np.repeat(cr, grid_w, axis=1)[:, None, :], F32)
    sre = jnp.asarray(np.repeat(sr, grid_w, axis=1)[:, None, :], F32)
    cct = jnp.asarray(np.tile(cc, (1, rows)), F32)
    sct = jnp.asarray(np.tile(sc, (1, rows)), F32)
    rpt = tm // grid_w
    return pl.pallas_call(
        _fourier_pos_kernel,
        grid=(t // tm, b),
        in_specs=[pl.BlockSpec((rpt, 1, t), lambda j, i: (j, 0, 0)),
                  pl.BlockSpec((rpt, 1, t), lambda j, i: (j, 0, 0)),
                  pl.BlockSpec((grid_w, t), lambda j, i: (0, 0)),
                  pl.BlockSpec((grid_w, t), lambda j, i: (0, 0)),
                  pl.BlockSpec((1, t, w), lambda j, i: (i, 0, 0)),
                  pl.BlockSpec((1, t, w), lambda j, i: (i, 0, 0))],
        out_specs=pl.BlockSpec((1, tm, w), lambda j, i: (i, j, 0)),
        out_shape=jax.ShapeDtypeStruct((b, t, w), BF16),
        scratch_shapes=[pltpu.VMEM((tm, t), BF16), pltpu.VMEM((tm, t), BF16)],
        compiler_params=_cparams(("arbitrary", "arbitrary"), VMEM_LIMIT),
        name="fourier_pos",
    )(cre, sre, cct, sct, a, bm)


def _out_kernel(hm_ref, fu_ref, x_ref, mod_ref, g_ref, wa_ref, wb_ref, rw_ref,
                x1_ref, aug_ref, afft_ref):
    a = _dot(hm_ref[...], wa_ref[...]) + _dot(fu_ref[...], wb_ref[...])
    x1 = x_ref[...] + mod_ref[0, 2:3, :] * a
    x1_ref[...] = x1
    ms = jnp.mean(x1 * x1, axis=-1, keepdims=True)
    y = x1 * lax.rsqrt(ms + EPS) * g_ref[...]
    h2 = y * (1.0 + mod_ref[0, 4:5, :]) + mod_ref[0, 3:4, :]
    logits = _dot3(h2, rw_ref[...])
    lane = lax.broadcasted_iota(I32, logits.shape, 1)
    valid = lane < N_EXPERTS
    lg = jnp.where(valid, logits, -1e30)
    ex = jnp.where(valid, jnp.exp(lg - jnp.max(lg, axis=1, keepdims=True)), 0.0)
    aff = ex / jnp.sum(ex, axis=1, keepdims=True)
    aug_ref[:, :D_MODEL] = h2
    aug_ref[:, D_MODEL:] = aff
    afft_ref[...] = aff.T[:N_EXPERTS, :]


def _out_proj(hm2, fu2, x2, mod, row0, tiles_per_batch, tm, g2, wa, wb, rw):
    n = x2.shape[0]
    const = lambda i: (0, 0)
    return pl.pallas_call(
        _out_kernel,
        grid=(n // tm,),
        in_specs=[pl.BlockSpec((tm, MLSTM_WIDTH), lambda i: (i, 0)),
                  pl.BlockSpec((tm, FOURIER_WIDTH), lambda i: (i, 0)),
                  pl.BlockSpec((tm, D_MODEL), lambda i: (i, 0)),
                  pl.BlockSpec((1, N_MOD, D_MODEL), _mod_index(row0, tiles_per_batch)),
                  pl.BlockSpec((1, D_MODEL), const),
                  pl.BlockSpec((MLSTM_WIDTH, D_MODEL), const),
                  pl.BlockSpec((FOURIER_WIDTH, D_MODEL), const),
                  pl.BlockSpec((D_MODEL, LANES), const)],
        out_specs=[pl.BlockSpec((tm, D_MODEL), lambda i: (i, 0)),
                   pl.BlockSpec((tm, AUG), lambda i: (i, 0)),
                   pl.BlockSpec((N_EXPERTS, tm), lambda i: (0, i))],
        out_shape=[jax.ShapeDtypeStruct((n, D_MODEL), F32),
                   jax.ShapeDtypeStruct((n, AUG), F32),
                   jax.ShapeDtypeStruct((N_EXPERTS, n), F32)],
        compiler_params=_cparams(("arbitrary",), VMEM_LIMIT),
        name="out_proj",
    )(hm2, fu2, x2, mod, g2, wa, wb, rw)


def _topk_kernel(aff_ref, pos_ref, blk_ref, *, n, cap):
    nb = n // LANES
    aff = aff_ref[...]

    def enough(t):
        return jnp.sum(jnp.where(aff >= t, 1.0, 0.0), axis=1, keepdims=True) >= cap

    def pow2(k):
        return lax.bitcast_convert_type(jnp.left_shift(127 - k, 23), F32)

    def exp_search(_, c):
        lo, hi = c
        mid = jnp.right_shift(lo + hi, 1)
        ok = enough(pow2(jnp.minimum(mid, 126)))
        return jnp.where(ok, lo, mid + 1), jnp.where(ok, mid, hi)

    kz = jnp.zeros((N_EXPERTS, 1), I32)
    kstar, _ = lax.fori_loop(0, 7, exp_search, (kz, kz + 127))
    found = kstar < 127
    p = pow2(jnp.minimum(kstar, 126))
    t_lo0 = jnp.where(found, p, 0.0)
    t_hi0 = jnp.where(found, 2.0 * p, p)

    def bisect(_, c):
        t_lo, t_hi = c
        mid = 0.5 * (t_lo + t_hi)
        ok = enough(mid)
        return jnp.where(ok, mid, t_lo), jnp.where(ok, t_hi, mid)

    t_lo, t_hi = lax.fori_loop(0, 40, bisect, (t_lo0, t_hi0))
    n_gt = jnp.sum(jnp.where(aff >= t_hi, 1.0, 0.0), axis=1, keepdims=True)
    need = cap - n_gt
    row = lax.broadcasted_iota(I32, (LANES, LANES), 0)
    col = lax.broadcasted_iota(I32, (LANES, LANES), 1)
    upper = (row <= col).astype(BF16)
    blane = lax.broadcasted_iota(I32, blk_ref.shape, 1)

    def block(b, carry):
        eq_off, sel_off = carry
        sl = pl.ds(pl.multiple_of(b * LANES, LANES), LANES)
        ab = aff_ref[:, sl]
        gt = ab >= t_hi
        eq = (ab >= t_lo) & (ab < t_hi)
        eqf = jnp.where(eq, 1.0, 0.0)
        eq_rank = _dot(eqf.astype(BF16), upper) + eq_off - eqf
        sel = gt | (eq & (eq_rank < need))
        self_ = jnp.where(sel, 1.0, 0.0)
        cum = _dot(self_.astype(BF16), upper) + sel_off
        pos_ref[:, sl] = jnp.where(sel, cum - 1.0, -1.0).astype(I32)
        blk_ref[...] = jnp.where(blane == b, sel_off.astype(I32), blk_ref[...])
        return (eq_off + jnp.sum(eqf, axis=1, keepdims=True),
                sel_off + jnp.sum(self_, axis=1, keepdims=True))

    blk_ref[...] = jnp.full(blk_ref.shape, cap, I32)
    zero = jnp.zeros((N_EXPERTS, 1), F32)
    lax.fori_loop(0, nb, block, (zero, zero))


def _topk(aff_t, cap):
    n = aff_t.shape[1]
    return pl.pallas_call(
        functools.partial(_topk_kernel, n=n, cap=cap),
        grid=(1,),
        in_specs=[pl.BlockSpec((N_EXPERTS, n), lambda i: (0, 0))],
        out_specs=[pl.BlockSpec((N_EXPERTS, n), lambda i: (0, 0)),
                   pl.BlockSpec((N_EXPERTS, 2 * LANES), lambda i: (0, 0))],
        out_shape=[jax.ShapeDtypeStruct((N_EXPERTS, n), I32),
                   jax.ShapeDtypeStruct((N_EXPERTS, 2 * LANES), I32)],
        compiler_params=_cparams(("arbitrary",)),
        name="topk",
    )(aff_t)


def _compact_kernel(blk_sm, pos_ref, out_ref, *, nb):
    e = pl.program_id(0)

    slot0 = lax.broadcasted_iota(I32, (LANES, LANES), 0)
    tok = lax.broadcasted_iota(I32, (LANES, LANES), 1)

    def chunk(jc, carry):
        first, last = carry
        j0 = jc * LANES
        first = lax.while_loop(lambda b: blk_sm[e, b + 1] <= j0, lambda b: b + 1, first)
        last = lax.while_loop(lambda b: (b < nb) & (blk_sm[e, b] < j0 + LANES), lambda b: b + 1, last)
        slot = slot0 + j0

        def body(b, acc):
            prow = pos_ref[0, :, pl.ds(pl.multiple_of(b * LANES, LANES), LANES)]
            return acc + jnp.where(prow == slot, (tok + b * LANES).astype(F32), 0.0)

        acc = lax.fori_loop(first, last, body, jnp.zeros((LANES, LANES), F32))
        out_ref[0, jc] = jnp.sum(acc.T, axis=0, keepdims=True).astype(I32)
        return first, last

    lax.fori_loop(0, out_ref.shape[1], chunk, (jnp.int32(0), jnp.int32(0)))


def _compact(pos, blk, cap):
    n = pos.shape[1]
    ncj = cap // LANES
    out = pl.pallas_call(
        functools.partial(_compact_kernel, nb=n // LANES),
        grid_spec=pltpu.PrefetchScalarGridSpec(
            num_scalar_prefetch=1,
            grid=(N_EXPERTS,),
            in_specs=[pl.BlockSpec((1, 1, n), lambda e, blk: (e, 0, 0))],
            out_specs=pl.BlockSpec((1, ncj, 1, LANES), lambda e, blk: (e, 0, 0, 0)),
        ),
        out_shape=jax.ShapeDtypeStruct((N_EXPERTS, ncj, 1, LANES), I32),
        compiler_params=_cparams(("arbitrary",)),
        name="compact",
    )(blk, pos.reshape(N_EXPERTS, 1, n))
    return out.reshape(N_EXPERTS, cap)


def _moe_kernel(lists_sm, aug_c, aug_l, w1_ref, w3_ref, w2_ref, ye_ref,
                xbuf, xe, gate, act_all, w2_all, w1b, w3b, sem, *, cap_c, cap_l, row_chunk, unroll, tf):
    e = pl.program_id(0)
    f = pl.program_id(1)
    ne = pl.num_programs(0)
    nf = pl.num_programs(1)
    rows = cap_c + cap_l

    def gather_rows(expert):
        base = expert * rows

        def body_for(aug):
            def body(g, carry):
                for u in range(unroll):
                    j = g * unroll + u
                    pltpu.make_async_copy(aug.at[pl.ds(lists_sm[base + j], 1)], xbuf.at[pl.ds(j, 1)],
                                          sem.at[0]).start()
                return carry
            return body

        lax.fori_loop(0, cap_c // unroll, body_for(aug_c), 0)
        lax.fori_loop(cap_c // unroll, rows // unroll, body_for(aug_l), 0)

    @pl.when((e == 0) & (f == 0))
    def _():
        gather_rows(e)

    @pl.when(f == 0)
    def _():
        pltpu.make_async_copy(xbuf, xbuf, sem.at[0]).wait()
        xe[...] = xbuf[:, :D_MODEL].astype(BF16)
        lane = lax.broadcasted_iota(I32, (rows, LANES), 1)
        gate[...] = jnp.sum(jnp.where(lane == e, xbuf[:, D_MODEL:], 0.0), axis=1, keepdims=True)
        ye_ref[0, rows:, :] = jnp.zeros((ye_ref.shape[1] - rows, D_MODEL), BF16)

        @pl.when(e + 1 < ne)
        def _():
            gather_rows(e + 1)

    w1b[...] = w1_ref[0].astype(BF16)
    w3b[...] = w3_ref[0].astype(BF16)
    fsl = pl.ds(pl.multiple_of(f * tf, tf), tf)
    w2_all[fsl, :] = w2_ref[0].astype(BF16)
    for r in range(rows // row_chunk):
        sl = pl.ds(r * row_chunk, row_chunk)
        x = xe[sl, :]
        a = _dot(x, w1b[...])
        bg = _dot(x, w3b[...])
        act_all[sl, fsl] = (a * jax.nn.sigmoid(a) * bg).astype(BF16)

    @pl.when(f == nf - 1)
    def _():
        for r in range(rows // row_chunk):
            sl = pl.ds(r * row_chunk, row_chunk)
            ye_ref[0, sl, :] = (_dot(act_all[sl, :], w2_all[...]) * gate[sl, :]).astype(BF16)


def _moe(lists, aug_c, aug_l, w1, w3, w2, cap_c, cap_l, tf=256, row_chunk=512, unroll=8):
    rows = cap_c + cap_l
    any_spec = pl.BlockSpec(memory_space=pl.ANY)
    return pl.pallas_call(
        functools.partial(_moe_kernel, cap_c=cap_c, cap_l=cap_l, row_chunk=row_chunk, unroll=unroll,
                          tf=tf),
        grid_spec=pltpu.PrefetchScalarGridSpec(
            num_scalar_prefetch=1,
            grid=(N_EXPERTS, D_EXPERT // tf),
            in_specs=[any_spec, any_spec,
                      pl.BlockSpec((1, D_MODEL, tf), lambda e, f, ls: (e, 0, f)),
                      pl.BlockSpec((1, D_MODEL, tf), lambda e, f, ls: (e, 0, f)),
                      pl.BlockSpec((1, tf, D_MODEL), lambda e, f, ls: (e, f, 0))],
            out_specs=pl.BlockSpec((1, rows + WINDOW, D_MODEL), lambda e, f, ls: (e, 0, 0)),
            scratch_shapes=[pltpu.VMEM((rows, AUG), F32), pltpu.VMEM((rows, D_MODEL), BF16),
                            pltpu.VMEM((rows, 1), F32),
                            pltpu.VMEM((rows, D_EXPERT), BF16), pltpu.VMEM((D_EXPERT, D_MODEL), BF16),
                            pltpu.VMEM((D_MODEL, tf), BF16), pltpu.VMEM((D_MODEL, tf), BF16),
                            pltpu.SemaphoreType.DMA((1,))],
        ),
        out_shape=jax.ShapeDtypeStruct((N_EXPERTS, rows + WINDOW, D_MODEL), BF16),
        compiler_params=_cparams(("arbitrary", "arbitrary"), VMEM_LIMIT),
        name="moe",
    )(lists, aug_c, aug_l, w1, w3, w2)


def _combine_kernel(seg_sm, x1_ref, pos_ref, mod_ref, g_ref, ye_hbm, o_ref, wbuf, sem, *, base, tm):
    i = pl.program_id(0)
    nt = pl.num_programs(0)
    bpt = tm // LANES
    slot = lax.rem(i, 2)
    last_start = ye_hbm.shape[1] - WINDOW
    tn = (((0,), (0,)), ((), ()))

    def start_windows(tile, rnd, buf):
        for e in range(N_EXPERTS):
            start = jnp.minimum(base + (seg_sm[e, tile * bpt] & -ROW_ALIGN) + rnd * WINDOW, last_start)
            pltpu.make_async_copy(ye_hbm.at[e, pl.ds(pl.multiple_of(start, ROW_ALIGN), WINDOW)],
                                  wbuf.at[buf, pl.ds(e * WINDOW, WINDOW)], sem.at[buf]).start()

    def wait_windows(buf):
        pltpu.make_async_copy(wbuf.at[buf], wbuf.at[buf], sem.at[buf]).wait()

    @pl.when(i == 0)
    def _():
        start_windows(i, 0, slot)

    @pl.when(i + 1 < nt)
    def _():
        start_windows(i + 1, 0, 1 - slot)

    most = jnp.int32(0)
    for e in range(N_EXPERTS):
        most = jnp.maximum(most, seg_sm[e, (i + 1) * bpt] - (seg_sm[e, i * bpt] & -ROW_ALIGN))
    n_rounds = jnp.right_shift(most + (WINDOW - 1), WINDOW_SHIFT)

    pos = pos_ref[...]
    valid = pos >= 0
    first = jnp.min(jnp.where(valid, pos, jnp.int32(2 ** 30)), axis=1, keepdims=True)
    rel = pos - (first & -ROW_ALIGN)
    lane = lax.broadcasted_iota(I32, (N_EXPERTS, N_EXPERTS * WINDOW), 1)
    owner = lax.broadcasted_iota(I32, (N_EXPERTS, N_EXPERTS * WINDOW), 0)
    spread = jnp.where(jnp.right_shift(lane, WINDOW_SHIFT) == owner, 1.0, 0.0).astype(BF16)
    wrow = (lax.broadcasted_iota(I32, (1, N_EXPERTS * WINDOW), 1) & (WINDOW - 1)).astype(F32)

    def place(rnd, y):
        q = jnp.where(valid & (rel >= rnd * WINDOW) & (rel < (rnd + 1) * WINDOW), rel - rnd * WINDOW, -1)
        qb = lax.dot_general(q.astype(F32).astype(BF16), spread, tn, preferred_element_type=F32)
        onehot = jnp.where(qb == wrow, 1.0, 0.0).astype(BF16)
        return y + _dot(onehot, wbuf[slot])

    wait_windows(slot)
    y = place(0, jnp.zeros((tm, D_MODEL), F32))

    def extra_round(rnd, y):
        start_windows(i, rnd, slot)
        wait_windows(slot)
        return place(rnd, y)

    y = lax.fori_loop(1, n_rounds, extra_round, y)
    x = x1_ref[...] + mod_ref[0, 5:6, :] * y
    ms = jnp.mean(x * x, axis=-1, keepdims=True)
    o_ref[...] = x * lax.rsqrt(ms + EPS) * g_ref[...]


def _combine_norm(seg, x1, pos, ye, base, mod, row0, tiles_per_batch, tm, gf):
    n = x1.shape[0]
    return pl.pallas_call(
        functools.partial(_combine_kernel, base=base, tm=tm),
        grid_spec=pltpu.PrefetchScalarGridSpec(
            num_scalar_prefetch=1,
            grid=(n // tm,),
            in_specs=[pl.BlockSpec((tm, D_MODEL), lambda i, sg: (i, 0)),
                      pl.BlockSpec((N_EXPERTS, tm), lambda i, sg: (0, i)),
                      pl.BlockSpec((1, N_MOD, D_MODEL),
                                   lambda i, sg: (row0 + i // tiles_per_batch, 0, 0)),
                      pl.BlockSpec((1, D_MODEL), lambda i, sg: (0, 0)),
                      pl.BlockSpec(memory_space=pl.ANY)],
            out_specs=pl.BlockSpec((tm, D_MODEL), lambda i, sg: (i, 0)),
            scratch_shapes=[pltpu.VMEM((2, N_EXPERTS * WINDOW, D_MODEL), BF16),
                            pltpu.SemaphoreType.DMA((2,))],
        ),
        out_shape=jax.ShapeDtypeStruct((n, D_MODEL), F32),
        compiler_params=_cparams(("arbitrary",), VMEM_LIMIT),
        name="combine_norm",
    )(seg, x1, pos, mod, gf, ye)


def kernel(x_prompt, x_sample, state_C, state_n, state_m, c, c_ctx, ada_w, ada_b, norm1_g, norm2_g,
           w_in, b_in, head_g, w_out, router_w, exp_w1, exp_w3, exp_w2, final_g):
    depth = ada_w.shape[0]
    assert depth == 1, "single-layer trunk"
    bc_, tc_, _ = x_prompt.shape
    bl_, tl_, _ = x_sample.shape
    w4 = 4 * MLSTM_WIDTH
    tm = 512

    cvec = jnp.concatenate([c_ctx[None, :], c, jnp.zeros((8 - 1 - bl_, D_MODEL), F32)], axis=0)
    mod = _modulation(cvec, ada_w[0], ada_b[0])

    wi = w_in[0]
    bi = b_in[0]
    wq = wi[:, :w4].astype(BF16)
    bq = bi[None, :w4]
    wg = jnp.pad(wi[:, w4:w4 + 16], ((0, 0), (0, LANES - 16))).astype(BF16)
    bg = jnp.pad(bi[w4:w4 + 16], (0, LANES - 16))[None, :]
    wf = wi[:, w4 + 16:].astype(BF16)
    bfo = bi[None, w4 + 16:]
    colscale = jnp.ones((1, w4), F32).at[:, MLSTM_WIDTH:2 * MLSTM_WIDTH].set(HEAD_DIM ** -0.5)
    g1 = norm1_g[0][None, :]
    g2 = norm2_g[0][None, :]
    wo = w_out[0].astype(BF16)
    wa, wb = wo[:MLSTM_WIDTH], wo[MLSTM_WIDTH:]
    rw = jnp.pad(router_w[0], ((0, 0), (0, LANES - N_EXPERTS)))
    hg3 = head_g[0][:, None, :]

    def spread(v):
        return jnp.broadcast_to(v[..., None, None], v.shape + (1, LANES))

    def mixer_and_router(x3, row0, init, grid_w):
        b, t, _ = x3.shape
        x2 = x3.reshape(b * t, D_MODEL)
        tpb = max(t // tm, 1) if row0 else (b * t) // tm + 1
        qkvo, gates, u = _in_proj(x2, mod, row0, tpb, tm, g1, wq, bq, colscale, wg, bg, wf, bfo)
        gate_rows = _gate_sums(gates).reshape(b, t // CHUNK, 2 * N_HEADS, GATE_ROWS, CHUNK)
        c0, n0, m0 = init
        hm, cf, nf, mf = _mlstm(qkvo.reshape(b, t, w4), gate_rows, c0, n0, m0, hg3)
        u3 = u.reshape(b, t, FOURIER_WIDTH)
        fu = _fourier_ctx(u3) if grid_w is None else _fourier_lat(u3, grid_w)
        x1, aug, aff_t = _out_proj(hm.reshape(b * t, MLSTM_WIDTH), fu.reshape(b * t, FOURIER_WIDTH),
                                   x2, mod, row0, tpb, tm, g2, wa, wb, rw)
        cap = CAPACITY_FACTOR * (b * t) // N_EXPERTS
        pos, blk = _topk(aff_t, cap)
        return x1, aug, _compact(pos, blk, cap), cap, (cf, nf, mf), (pos, blk)

    zero_init = (jnp.zeros((bc_, 2, N_HEADS, HEAD_DIM, HEAD_DIM), F32),
                 jnp.zeros((bc_, 2, N_HEADS, 1, LANES), F32),
                 jnp.zeros((bc_, 2, N_HEADS, 1, LANES), F32))
    lat_init = (state_C[:, 0], state_n[:, 0][..., None, :], spread(state_m[:, 0]))
    x1c, augc, listc, capc, (cf, nf, mf), (posc, blkc) = mixer_and_router(x_prompt, 0, zero_init, None)
    x1l, augl, listl, capl, _, (posl, blkl) = mixer_and_router(x_sample, 1, lat_init, GRID_W)

    lists = jnp.concatenate([listc, listl], axis=1).reshape(-1)
    ye = _moe(lists, augc, augl, exp_w1[0], exp_w3[0], exp_w2[0], capc, capl)
    gf = final_g[None, :]
    tc = 256
    y_prompt = _combine_norm(blkc, x1c, posc, ye, 0, mod, 0, bc_ * tc_ // tc + 1, tc, gf)
    y_sample = _combine_norm(blkl, x1l, posl, ye, capc, mod, 1, tl_ // tc, tc, gf)
    y_prompt = y_prompt.reshape(x_prompt.shape)
    y_sample = y_sample.reshape(x_sample.shape)
    new_c = cf[:, None]
    new_n = nf[:, None, :, :, 0, :]
    new_m = mf[:, None, :, :, 0, 0]
    return (y_prompt, y_sample, new_c, new_n, new_m)
```

```python
import functools
import math

import numpy as np
import jax
import jax.numpy as jnp
from jax import lax
from jax.experimental import pallas as pl
from jax.experimental.pallas import tpu as pltpu

F32 = jnp.float32
BF16 = jnp.bfloat16
I32 = jnp.int32

D_MODEL = 1024
N_HEADS = 4
HEAD_DIM = 128
MLSTM_WIDTH = N_HEADS * HEAD_DIM
FOURIER_WIDTH = 512
N_GROUPS = 4
GROUP_DIM = 128
CHUNK = 128
GRID_W = 64
N_EXPERTS = 16
CAPACITY_FACTOR = 2
D_EXPERT = 2048
N_MOD = 6
EPS = 1e-6
LANES = 128
GATE_ROWS = 8
STATE_PAD = 16
STATE_ROWS = HEAD_DIM + STATE_PAD
ROW_ALIGN = 16
AUG = D_MODEL + LANES
WINDOW_SHIFT = 6
WINDOW = 1 << WINDOW_SHIFT
VMEM_LIMIT = 56 * 1024 * 1024


def _cparams(sem, vmem=None):
    return pltpu.CompilerParams(dimension_semantics=sem, vmem_limit_bytes=vmem)


def _split3(a):
    a1 = a.astype(BF16)
    r1 = a - a1.astype(F32)
    a2 = r1.astype(BF16)
    a3 = (r1 - a2.astype(F32)).astype(BF16)
    return a1, a2, a3


def _dot(a, b):
    return jnp.dot(a, b, preferred_element_type=F32)


def _dot3(a, b):
    a1, a2, _ = _split3(a)
    b1, b2, _ = _split3(b)
    return _dot(a1, b1) + (_dot(a1, b2) + _dot(a2, b1))


def _mod_kernel(c_ref, w_ref, b_ref, o_ref):
    c = c_ref[...]
    s = c * jax.nn.sigmoid(c)
    o_ref[...] = _dot(s.astype(BF16), w_ref[...].astype(BF16)) + b_ref[...]


def _modulation(cvec8, ada_w, ada_b):
    tn = D_MODEL
    out = pl.pallas_call(
        _mod_kernel,
        grid=(N_MOD,),
        in_specs=[pl.BlockSpec((8, D_MODEL), lambda j: (0, 0)),
                  pl.BlockSpec((D_MODEL, tn), lambda j: (0, j)),
                  pl.BlockSpec((1, tn), lambda j: (0, j))],
        out_specs=pl.BlockSpec((8, tn), lambda j: (0, j)),
        out_shape=jax.ShapeDtypeStruct((8, N_MOD * D_MODEL), F32),
        compiler_params=_cparams(("arbitrary",)),
        name="modulation",
    )(cvec8, ada_w, ada_b.reshape(1, -1))
    return out.reshape(8, N_MOD, D_MODEL)


def _mod_index(row0, tiles_per_batch):
    return lambda i: (row0 + i // tiles_per_batch, 0, 0)


def _in_kernel(x_ref, mod_ref, g_ref, wq_ref, bq_ref, sc_ref, wg_ref, bg_ref, wf_ref, bf_ref,
               qkvo_ref, gates_ref, u_ref):
    x = x_ref[...]
    ms = jnp.mean(x * x, axis=-1, keepdims=True)
    y = x * lax.rsqrt(ms + EPS) * g_ref[...]
    h = y * (1.0 + mod_ref[0, 1:2, :]) + mod_ref[0, 0:1, :]
    hb = h.astype(BF16)
    p = (_dot(hb, wq_ref[...]) + bq_ref[...]) * sc_ref[...]
    qkvo_ref[...] = p.astype(BF16)
    gates_ref[...] = _dot(hb, wg_ref[...]) + bg_ref[...]
    u_ref[...] = _dot(hb, wf_ref[...]) + bf_ref[...]


def _in_proj(x2, mod, row0, tiles_per_batch, tm, g1, wq, bq, sc, wg, bg, wf, bfo):
    n = x2.shape[0]
    wq_n = wq.shape[1]
    const = lambda i: (0, 0)
    return pl.pallas_call(
        _in_kernel,
        grid=(n // tm,),
        in_specs=[pl.BlockSpec((tm, D_MODEL), lambda i: (i, 0)),
                  pl.BlockSpec((1, N_MOD, D_MODEL), _mod_index(row0, tiles_per_batch)),
                  pl.BlockSpec((1, D_MODEL), const),
                  pl.BlockSpec((D_MODEL, wq_n), const), pl.BlockSpec((1, wq_n), const),
                  pl.BlockSpec((1, wq_n), const),
                  pl.BlockSpec((D_MODEL, LANES), const), pl.BlockSpec((1, LANES), const),
                  pl.BlockSpec((D_MODEL, FOURIER_WIDTH), const), pl.BlockSpec((1, FOURIER_WIDTH), const)],
        out_specs=[pl.BlockSpec((tm, wq_n), lambda i: (i, 0)),
                   pl.BlockSpec((tm, LANES), lambda i: (i, 0)),
                   pl.BlockSpec((tm, FOURIER_WIDTH), lambda i: (i, 0))],
        out_shape=[jax.ShapeDtypeStruct((n, wq_n), BF16),
                   jax.ShapeDtypeStruct((n, LANES), F32),
                   jax.ShapeDtypeStruct((n, FOURIER_WIDTH), F32)],
        compiler_params=_cparams(("arbitrary",), VMEM_LIMIT),
        name="in_proj",
    )(x2, mod, g1, wq, bq, sc, wg, bg, wf, bfo)


def _gates_kernel(g_ref, out_ref):
    row = lax.broadcasted_iota(I32, (CHUNK, CHUNK), 0)
    col = lax.broadcasted_iota(I32, (CHUNK, CHUNK), 1)
    tril = (col <= row).astype(BF16)
    triu = (col >= row).astype(BF16)
    lane = lax.broadcasted_iota(I32, (CHUNK, LANES), 1)
    out_ref[...] = jnp.zeros(out_ref.shape, F32)
    for c in range(out_ref.shape[0]):
        g = g_ref[pl.ds(c * CHUNK, CHUNK), :]
        ls = jnp.minimum(g, 0.0) - jnp.log(1.0 + jnp.exp(-jnp.abs(g)))
        l1, l2, l3 = _split3(ls)
        pre = _dot(tril, l1) + _dot(tril, l2) + _dot(tril, l3)
        suf = _dot(triu, l1) + _dot(triu, l2) + _dot(triu, l3)
        bct = jnp.where((lane >= 4) & (lane < 8), pre,
                        jnp.where((lane >= 12) & (lane < 16), suf, g)).T
        for head in range(N_HEADS):
            for d in range(2):
                ich = 2 * N_HEADS * d + head
                fch = ich + N_HEADS
                out_ref[c, 2 * head + d, 0:1, :] = bct[ich:ich + 1, :]
                out_ref[c, 2 * head + d, 1:2, :] = bct[fch:fch + 1, :]


def _gate_sums(gates, chunks_per_step=8):
    n = gates.shape[0]
    nc = n // CHUNK
    cps = chunks_per_step
    return pl.pallas_call(
        _gates_kernel,
        grid=(nc // cps,),
        in_specs=[pl.BlockSpec((cps * CHUNK, LANES), lambda i: (i, 0))],
        out_specs=pl.BlockSpec((cps, 2 * N_HEADS, GATE_ROWS, CHUNK), lambda i: (i, 0, 0, 0)),
        out_shape=jax.ShapeDtypeStruct((nc, 2 * N_HEADS, GATE_ROWS, CHUNK), F32),
        compiler_params=_cparams(("arbitrary",)),
        name="gate_sums",
    )(gates)


def _mlstm_kernel(qkvo_ref, g_ref, c0_ref, n0_ref, m0_ref, hg_ref,
                  hm_ref, cf_ref, nf_ref, mf_ref, hf_s, hb_s, c_s, m_s, *, n_chunks):
    c_s[:, :, :HEAD_DIM, :] = c0_ref[0]
    c_s[:, :, HEAD_DIM:, :] = jnp.broadcast_to(n0_ref[0], (2, N_HEADS, STATE_PAD, HEAD_DIM))
    m_s[...] = m0_ref[0]
    src = lax.broadcasted_iota(I32, (CHUNK, CHUNK), 0)
    dst = lax.broadcasted_iota(I32, (CHUNK, CHUNK), 1)
    nt = (((1,), (1,)), ((), ()))

    zero_blk = jnp.zeros((CHUNK, HEAD_DIM), BF16)
    zero_pad = jnp.zeros((STATE_ROWS, HEAD_DIM), BF16)

    def diag2(x_a, x_b, zero):
        return jnp.concatenate([jnp.concatenate([x_a, zero], axis=1),
                                jnp.concatenate([zero, x_b], axis=1)], axis=0)

    def pair_direction(c, pair, d, prev):
        sl = pl.ds(pl.multiple_of(c * CHUNK, CHUNK), CHUNK)
        cols = lambda part: pl.ds((part * N_HEADS + 2 * pair) * HEAD_DIM, 2 * HEAD_DIM)
        q2 = qkvo_ref[0, sl, cols(0)]
        k2 = qkvo_ref[0, sl, cols(1)]
        vt2 = qkvo_ref[0, sl, cols(2)].astype(F32).T
        k_ab = [k2[:, :HEAD_DIM], k2[:, HEAD_DIM:]]
        vt_ab = [vt2[:HEAD_DIM, :], vt2[HEAD_DIM:, :]]
        mask = (src <= dst) if d == 0 else (src >= dst)
        rows = []
        for j in range(2):
            gates = g_ref[0, c, 2 * (2 * pair + j) + d]
            ig_row, b_row = gates[0:1, :], gates[1:2, :]
            gates_t = gates.T
            g = b_row[:, CHUNK - 1:CHUNK] if d == 0 else b_row[:, 0:1]
            w_log = g - b_row + ig_row
            m_loc = jnp.max(w_log, axis=1, keepdims=True)
            w = jnp.exp(w_log - m_loc)
            rows.append((ig_row, b_row, gates_t[:, 0:1], gates_t[:, 1:2], g, m_loc, w))
        c_bf = [prev[j][0].astype(BF16) for j in range(2)]
        lhs_q = jnp.concatenate([diag2(k_ab[0], k_ab[1], zero_blk), diag2(c_bf[0], c_bf[1], zero_pad)], axis=0)
        out_q = lax.dot_general(lhs_q, q2, nt, preferred_element_type=F32)
        s_list, extra = [], []
        for j in range(2):
            ig_row, b_row, ig_col, b_col, g, m_loc, w = rows[j]
            m_prev = prev[j][1]
            dmat = jnp.where(mask, b_row - b_col + ig_col, -jnp.inf)
            inter_log = b_row + m_prev
            m_comb = jnp.maximum(inter_log, jnp.max(dmat, axis=0, keepdims=True))
            s_t = out_q[j * CHUNK:(j + 1) * CHUNK, :] * jnp.exp(dmat - m_comb)
            s_list.append(s_t)
            extra.append((m_comb, jnp.exp(inter_log - m_comb)))
        s2 = jnp.concatenate(s_list, axis=0).astype(BF16)
        num2 = _dot(diag2(vt_ab[0].astype(BF16), vt_ab[1].astype(BF16), zero_blk), s2)
        lhs_k = [jnp.concatenate([vt_ab[j] * rows[j][6], jnp.broadcast_to(rows[j][6], (STATE_PAD, CHUNK))],
                                 axis=0).astype(BF16) for j in range(2)]
        k_rows = jnp.concatenate(k_ab, axis=0)
        c_loc2 = _dot(diag2(lhs_k[0], lhs_k[1], jnp.zeros((STATE_ROWS, CHUNK), BF16)), k_rows)
        res = []
        for j in range(2):
            ig_row, b_row, ig_col, b_col, g, m_loc, w = rows[j]
            c_prev, m_prev = prev[j]
            m_comb, w_inter = extra[j]
            cq = out_q[2 * CHUNK + j * STATE_ROWS:2 * CHUNK + (j + 1) * STATE_ROWS, :]
            num = num2[j * HEAD_DIM:(j + 1) * HEAD_DIM, :] + w_inter * cq[:HEAD_DIM, :]
            den = jnp.sum(s_list[j], axis=0, keepdims=True) + w_inter * cq[HEAD_DIM:HEAD_DIM + 1, :]
            h = num / jnp.maximum(jnp.abs(den), jnp.exp(-m_comb))
            m_new = jnp.maximum(g + m_prev, m_loc)
            a = jnp.exp(g + m_prev - m_new)
            bb = jnp.exp(m_loc - m_new)
            c_loc = c_loc2[j * STATE_ROWS:(j + 1) * STATE_ROWS, :]
            res.append((h, a * c_prev + bb * c_loc, m_new))
        return res

    def scan_body(i, carry):
        units = [(pair, d) for pair in range(N_HEADS // 2) for d in range(2)]
        prev = [[(c_s[d, 2 * pair + j], m_s[d, 2 * pair + j][:, 0:1]) for j in range(2)] for pair, d in units]
        chunk = [i, n_chunks - 1 - i]
        new = [pair_direction(chunk[d], pair, d, prev[u]) for u, (pair, d) in enumerate(units)]
        for (pair, d), res in zip(units, new):
            sl = pl.ds(pl.multiple_of(chunk[d] * CHUNK, CHUNK), CHUNK)
            for j, (h, c_new, m_new) in enumerate(res):
                head = 2 * pair + j
                (hf_s, hb_s)[d][head, :, sl] = h
                c_s[d, head] = c_new
                m_s[d, head] = jnp.broadcast_to(m_new, (1, LANES))
        return carry

    lax.fori_loop(0, n_chunks, scan_body, 0)

    def out_body(c, carry):
        sl = pl.ds(pl.multiple_of(c * CHUNK, CHUNK), CHUNK)
        for head in range(N_HEADS):
            hs = hf_s[head, :, sl] + hb_s[head, :, sl]
            hn = (hs * lax.rsqrt(jnp.mean(hs * hs, axis=0, keepdims=True) + EPS)).T * hg_ref[head]
            o = qkvo_ref[0, sl, pl.ds((3 * N_HEADS + head) * HEAD_DIM, HEAD_DIM)].astype(F32)
            hm_ref[0, sl, pl.ds(head * HEAD_DIM, HEAD_DIM)] = (hn * jax.nn.sigmoid(o)).astype(BF16)
        return carry

    lax.fori_loop(0, n_chunks, out_body, 0)
    cf_ref[0] = c_s[:, :, :HEAD_DIM, :]
    nf_ref[0] = c_s[:, :, HEAD_DIM:HEAD_DIM + 1, :]
    mf_ref[0] = m_s[...]


def _mlstm(qkvo3, gates5, c0, n0, m0, head_g3):
    b, t, w4 = qkvo3.shape
    nc = t // CHUNK
    once = dict(pipeline_mode=pl.Buffered(1))
    st_c = pl.BlockSpec((1, 2, N_HEADS, HEAD_DIM, HEAD_DIM), lambda i: (i, 0, 0, 0, 0))
    st_v = pl.BlockSpec((1, 2, N_HEADS, 1, LANES), lambda i: (i, 0, 0, 0, 0))
    return pl.pallas_call(
        functools.partial(_mlstm_kernel, n_chunks=nc),
        grid=(b,),
        in_specs=[pl.BlockSpec((1, t, w4), lambda i: (i, 0, 0), **once),
                  pl.BlockSpec((1, nc, 2 * N_HEADS, GATE_ROWS, CHUNK), lambda i: (i, 0, 0, 0, 0), **once),
                  st_c, st_v, st_v,
                  pl.BlockSpec((N_HEADS, 1, HEAD_DIM), lambda i: (0, 0, 0))],
        out_specs=[pl.BlockSpec((1, t, MLSTM_WIDTH), lambda i: (i, 0, 0), **once), st_c, st_v, st_v],
        out_shape=[jax.ShapeDtypeStruct((b, t, MLSTM_WIDTH), BF16),
                   jax.ShapeDtypeStruct((b, 2, N_HEADS, HEAD_DIM, HEAD_DIM), F32),
                   jax.ShapeDtypeStruct((b, 2, N_HEADS, 1, LANES), F32),
                   jax.ShapeDtypeStruct((b, 2, N_HEADS, 1, LANES), F32)],
        scratch_shapes=[pltpu.VMEM((N_HEADS, HEAD_DIM, t), F32), pltpu.VMEM((N_HEADS, HEAD_DIM, t), F32),
                        pltpu.VMEM((2, N_HEADS, HEAD_DIM + STATE_PAD, HEAD_DIM), F32),
                        pltpu.VMEM((2, N_HEADS, 1, LANES), F32)],
        compiler_params=_cparams(("arbitrary",), VMEM_LIMIT),
        name="mlstm",
    )(qkvo3, gates5, c0, n0, m0, head_g3)


def _dft_cos_sin(n):
    k = np.arange(n)
    ang = 2.0 * np.pi * ((k[:, None] * k[None, :]) % n) / n
    return np.cos(ang), np.sin(ang)


def _channel_dft(scale):
    cd, sd = _dft_cos_sin(GROUP_DIM)
    eye = np.eye(N_GROUPS)
    return (jnp.asarray(np.kron(eye, cd) * scale, F32), jnp.asarray(np.kron(eye, -sd) * scale, F32))


def _fourier_ctx_kernel(u_ref, bdc_ref, bds_ref, ct_ref, st_ref, o_ref):
    u = u_ref[0]
    a = _dot3(u, bdc_ref[...])
    b = _dot3(u, bds_ref[...])
    o_ref[0] = (_dot3(ct_ref[...], a) + _dot3(st_ref[...], b)).astype(BF16)


def _fourier_ctx(u3):
    b, t, w = u3.shape
    bdc, bds = _channel_dft(1.0 / math.sqrt(t * GROUP_DIM))
    ct, st = _dft_cos_sin(t)
    const = lambda i: (0, 0)
    return pl.pallas_call(
        _fourier_ctx_kernel,
        grid=(b,),
        in_specs=[pl.BlockSpec((1, t, w), lambda i: (i, 0, 0)),
                  pl.BlockSpec((w, w), const), pl.BlockSpec((w, w), const),
                  pl.BlockSpec((t, t), const), pl.BlockSpec((t, t), const)],
        out_specs=pl.BlockSpec((1, t, w), lambda i: (i, 0, 0)),
        out_shape=jax.ShapeDtypeStruct((b, t, w), BF16),
        compiler_params=_cparams(("arbitrary",)),
        name="fourier_ctx",
    )(u3, bdc, bds, jnp.asarray(ct, F32), jnp.asarray(st, F32))


def _fourier_chan_kernel(u_ref, bdc_ref, bds_ref, a_ref, b_ref):
    u = u_ref[0]
    a_ref[0] = _dot3(u, bdc_ref[...]).astype(BF16)
    b_ref[0] = _dot3(u, bds_ref[...]).astype(BF16)


def _fourier_pos_kernel(cre_ref, sre_ref, cct_ref, sct_ref, a_ref, b_ref, o_ref, cp_s, sp_s):
    @pl.when(pl.program_id(1) == 0)
    def _():
        gw = cct_ref.shape[0]
        for q in range(cre_ref.shape[0]):
            cr, sr = cre_ref[q], sre_ref[q]
            cc, sc = cct_ref[...], sct_ref[...]
            cp_s[q * gw:(q + 1) * gw, :] = (cr * cc - sr * sc).astype(BF16)
            sp_s[q * gw:(q + 1) * gw, :] = (sr * cc + cr * sc).astype(BF16)

    o_ref[0] = (_dot(cp_s[...], a_ref[0]) + _dot(sp_s[...], b_ref[0])).astype(BF16)


def _fourier_lat(u3, grid_w):
    b, t, w = u3.shape
    rows = t // grid_w
    bdc, bds = _channel_dft(1.0 / math.sqrt(t * GROUP_DIM))
    tm = 512
    const = lambda i, j: (0, 0)
    a, bm = pl.pallas_call(
        _fourier_chan_kernel,
        grid=(b, t // tm),
        in_specs=[pl.BlockSpec((1, tm, w), lambda i, j: (i, j, 0)),
                  pl.BlockSpec((w, w), const), pl.BlockSpec((w, w), const)],
        out_specs=[pl.BlockSpec((1, tm, w), lambda i, j: (i, j, 0))] * 2,
        out_shape=[jax.ShapeDtypeStruct((b, t, w), BF16)] * 2,
        compiler_params=_cparams(("arbitrary", "arbitrary")),
        name="fourier_chan",
    )(u3, bdc, bds)
    cr, sr = _dft_cos_sin(rows)
    cc, sc = _dft_cos_sin(grid_w)
    cre = jnp.asarray(np.repeat(cr, grid_w, axis=1)[:, None, :], F32)
    sre = jnp.asarray(np.repeat(sr, grid_w, axis=1)[:, None, :], F32)
    cct = jnp.asarray(np.tile(cc, (1, rows)), F32)
    sct = jnp.asarray(np.tile(sc, (1, rows)), F32)
    rpt = tm // grid_w
    return pl.pallas_call(
        _fourier_pos_kernel,
        grid=(t // tm, b),
        in_specs=[pl.BlockSpec((rpt, 1, t), lambda j, i: (j, 0, 0)),
                  pl.BlockSpec((rpt, 1, t), lambda j, i: (j, 0, 0)),
                  pl.BlockSpec((grid_w, t), lambda j, i: (0, 0)),
                  pl.BlockSpec((grid_w, t), lambda j, i: (0, 0)),
                  pl.BlockSpec((1, t, w), lambda j, i: (i, 0, 0)),
                  pl.BlockSpec((1, t, w), lambda j, i: (i, 0, 0))],
        out_specs=pl.BlockSpec((1, tm, w), lambda j, i: (i, j, 0)),
        out_shape=jax.ShapeDtypeStruct((b, t, w), BF16),
        scratch_shapes=[pltpu.VMEM((tm, t), BF16), pltpu.VMEM((tm, t), BF16)],
        compiler_params=_cparams(("arbitrary", "arbitrary"), VMEM_LIMIT),
        name="fourier_pos",
    )(cre, sre, cct, sct, a, bm)


def _out_kernel(hmc_ref, fuc_ref, xc_ref, hml_ref, ful_ref, xl_ref, mod_ref, g_ref, wa_ref, wb_ref, rw_ref,
                x1_ref, aug_ref, afft_ref, *, ctx_tiles):
    def tile(hm_ref, fu_ref, x_ref):
        a = _dot(hm_ref[...], wa_ref[...]) + _dot(fu_ref[...], wb_ref[...])
        x1 = x_ref[...] + mod_ref[0, 2:3, :] * a
        x1_ref[...] = x1
        ms = jnp.mean(x1 * x1, axis=-1, keepdims=True)
        y = x1 * lax.rsqrt(ms + EPS) * g_ref[...]
        h2 = y * (1.0 + mod_ref[0, 4:5, :]) + mod_ref[0, 3:4, :]
        logits = _dot3(h2, rw_ref[...])
        lane = lax.broadcasted_iota(I32, logits.shape, 1)
        valid = lane < N_EXPERTS
        lg = jnp.where(valid, logits, -1e30)
        ex = jnp.where(valid, jnp.exp(lg - jnp.max(lg, axis=1, keepdims=True)), 0.0)
        aff = ex / jnp.sum(ex, axis=1, keepdims=True)
        aug_ref[:, :D_MODEL] = h2
        aug_ref[:, D_MODEL:] = aff
        afft_ref[...] = aff.T[:N_EXPERTS, :]

    @pl.when(pl.program_id(0) < ctx_tiles)
    def _():
        tile(hmc_ref, fuc_ref, xc_ref)

    @pl.when(pl.program_id(0) >= ctx_tiles)
    def _():
        tile(hml_ref, ful_ref, xl_ref)


def _out_proj(ctx, lat, mod, lat_tiles_per_batch, tm, g2, wa, wb, rw):
    n_c, n_l = ctx[2].shape[0], lat[2].shape[0]
    n = n_c + n_l
    ct = n_c // tm
    const = lambda i: (0, 0)
    ctx_tile = lambda i: (jnp.minimum(i, ct - 1), 0)
    lat_tile = lambda i: (jnp.maximum(i - ct, 0), 0)
    widths = (MLSTM_WIDTH, FOURIER_WIDTH, D_MODEL)
    return pl.pallas_call(
        functools.partial(_out_kernel, ctx_tiles=ct),
        grid=(n // tm,),
        in_specs=[pl.BlockSpec((tm, w), ctx_tile) for w in widths]
                 + [pl.BlockSpec((tm, w), lat_tile) for w in widths]
                 + [pl.BlockSpec((1, N_MOD, D_MODEL),
                                 lambda i: (jnp.where(i < ct, 0, 1 + (i - ct) // lat_tiles_per_batch), 0, 0)),
                    pl.BlockSpec((1, D_MODEL), const),
                    pl.BlockSpec((MLSTM_WIDTH, D_MODEL), const),
                    pl.BlockSpec((FOURIER_WIDTH, D_MODEL), const),
                    pl.BlockSpec((D_MODEL, LANES), const)],
        out_specs=[pl.BlockSpec((tm, D_MODEL), lambda i: (i, 0)),
                   pl.BlockSpec((tm, AUG), lambda i: (i, 0)),
                   pl.BlockSpec((N_EXPERTS, tm), lambda i: (0, i))],
        out_shape=[jax.ShapeDtypeStruct((n, D_MODEL), F32),
                   jax.ShapeDtypeStruct((n, AUG), F32),
                   jax.ShapeDtypeStruct((N_EXPERTS, n), F32)],
        compiler_params=_cparams(("arbitrary",), VMEM_LIMIT),
        name="out_proj",
    )(*ctx, *lat, mod, g2, wa, wb, rw)


def _topk_kernel(aff_ref, pos_ref, blk_ref, *, n, cap):
    nb = n // LANES
    aff = aff_ref[...]

    def enough(t):
        return jnp.sum(jnp.where(aff >= t, 1.0, 0.0), axis=1, keepdims=True) >= cap

    def pow2(k):
        return lax.bitcast_convert_type(jnp.left_shift(127 - k, 23), F32)

    def exp_search(_, c):
        lo, hi = c
        mid = jnp.right_shift(lo + hi, 1)
        ok = enough(pow2(jnp.minimum(mid, 126)))
        return jnp.where(ok, lo, mid + 1), jnp.where(ok, mid, hi)

    kz = jnp.zeros((N_EXPERTS, 1), I32)
    kstar, _ = lax.fori_loop(0, 7, exp_search, (kz, kz + 127))
    found = kstar < 127
    p = pow2(jnp.minimum(kstar, 126))
    t_lo0 = jnp.where(found, p, 0.0)
    t_hi0 = jnp.where(found, 2.0 * p, p)

    def bisect(_, c):
        t_lo, t_hi = c
        mid = 0.5 * (t_lo + t_hi)
        ok = enough(mid)
        return jnp.where(ok, mid, t_lo), jnp.where(ok, t_hi, mid)

    t_lo, t_hi = lax.fori_loop(0, 40, bisect, (t_lo0, t_hi0))
    n_gt = jnp.sum(jnp.where(aff >= t_hi, 1.0, 0.0), axis=1, keepdims=True)
    need = cap - n_gt
    row = lax.broadcasted_iota(I32, (LANES, LANES), 0)
    col = lax.broadcasted_iota(I32, (LANES, LANES), 1)
    upper = (row <= col).astype(BF16)
    blane = lax.broadcasted_iota(I32, blk_ref.shape, 1)

    def block(b, carry):
        eq_off, sel_off = carry
        sl = pl.ds(pl.multiple_of(b * LANES, LANES), LANES)
        ab = aff_ref[:, sl]
        gt = ab >= t_hi
        eq = (ab >= t_lo) & (ab < t_hi)
        eqf = jnp.where(eq, 1.0, 0.0)
        eq_rank = _dot(eqf.astype(BF16), upper) + eq_off - eqf
        sel = gt | (eq & (eq_rank < need))
        self_ = jnp.where(sel, 1.0, 0.0)
        cum = _dot(self_.astype(BF16), upper) + sel_off
        pos_ref[:, sl] = jnp.where(sel, cum - 1.0, -1.0).astype(I32)
        blk_ref[...] = jnp.where(blane == b, sel_off.astype(I32), blk_ref[...])
        return (eq_off + jnp.sum(eqf, axis=1, keepdims=True),
                sel_off + jnp.sum(self_, axis=1, keepdims=True))

    blk_ref[...] = jnp.full(blk_ref.shape, cap, I32)
    zero = jnp.zeros((N_EXPERTS, 1), F32)
    lax.fori_loop(0, nb, block, (zero, zero))


def _topk(aff_t, cap):
    n = aff_t.shape[1]
    return pl.pallas_call(
        functools.partial(_topk_kernel, n=n, cap=cap),
        grid=(1,),
        in_specs=[pl.BlockSpec((N_EXPERTS, n), lambda i: (0, 0))],
        out_specs=[pl.BlockSpec((N_EXPERTS, n), lambda i: (0, 0)),
                   pl.BlockSpec((N_EXPERTS, 2 * LANES), lambda i: (0, 0))],
        out_shape=[jax.ShapeDtypeStruct((N_EXPERTS, n), I32),
                   jax.ShapeDtypeStruct((N_EXPERTS, 2 * LANES), I32)],
        compiler_params=_cparams(("arbitrary",)),
        name="topk",
    )(aff_t)


def _compact_kernel(blk_sm, pos_ref, out_ref, *, nb):
    e = pl.program_id(0)

    slot0 = lax.broadcasted_iota(I32, (LANES, LANES), 0)
    tok = lax.broadcasted_iota(I32, (LANES, LANES), 1)

    def chunk(jc, carry):
        first, last = carry
        j0 = jc * LANES
        first = lax.while_loop(lambda b: blk_sm[e, b + 1] <= j0, lambda b: b + 1, first)
        last = lax.while_loop(lambda b: (b < nb) & (blk_sm[e, b] < j0 + LANES), lambda b: b + 1, last)
        slot = slot0 + j0

        def body(b, acc):
            prow = pos_ref[0, :, pl.ds(pl.multiple_of(b * LANES, LANES), LANES)]
            return acc + jnp.where(prow == slot, (tok + b * LANES).astype(F32), 0.0)

        acc = lax.fori_loop(first, last, body, jnp.zeros((LANES, LANES), F32))
        out_ref[0, jc] = jnp.sum(acc.T, axis=0, keepdims=True).astype(I32)
        return first, last

    lax.fori_loop(0, out_ref.shape[1], chunk, (jnp.int32(0), jnp.int32(0)))


def _compact(pos, blk, cap):
    n = pos.shape[1]
    ncj = cap // LANES
    out = pl.pallas_call(
        functools.partial(_compact_kernel, nb=n // LANES),
        grid_spec=pltpu.PrefetchScalarGridSpec(
            num_scalar_prefetch=1,
            grid=(N_EXPERTS,),
            in_specs=[pl.BlockSpec((1, 1, n), lambda e, blk: (e, 0, 0))],
            out_specs=pl.BlockSpec((1, ncj, 1, LANES), lambda e, blk: (e, 0, 0, 0)),
        ),
        out_shape=jax.ShapeDtypeStruct((N_EXPERTS, ncj, 1, LANES), I32),
        compiler_params=_cparams(("arbitrary",)),
        name="compact",
    )(blk, pos.reshape(N_EXPERTS, 1, n))
    return out.reshape(N_EXPERTS, cap)


def _moe_kernel(lists_sm, aug, w1_ref, w3_ref, w2_ref, ye_ref,
                xbuf, xe, gate, act_all, w2_all, w1b, w3b, sem, *, rows, row_chunk, unroll, tf, nf_static):
    e = pl.program_id(0)
    f = pl.program_id(1)
    ne = pl.num_programs(0)
    nf = pl.num_programs(1)

    def start_row(list_pos, j):
        pltpu.make_async_copy(aug.at[pl.ds(lists_sm[list_pos], 1)], xbuf.at[pl.ds(j, 1)], sem.at[0]).start()

    def wait_rows():
        pltpu.make_async_copy(xbuf, xbuf, sem.at[0]).wait()

    @pl.when((e == 0) & (f == 0))
    def _():
        def body(g, carry):
            for u in range(unroll):
                start_row(g * unroll + u, g * unroll + u)
            return carry

        lax.fori_loop(0, rows // unroll, body, 0)

    @pl.when(f == 0)
    def _():
        wait_rows()
        xe[...] = xbuf[:, :D_MODEL].astype(BF16)
        lane = lax.broadcasted_iota(I32, (rows, LANES), 1)
        gate[...] = jnp.sum(jnp.where(lane == e, xbuf[:, D_MODEL:], 0.0), axis=1, keepdims=True)
        ye_ref[0, rows:, :] = jnp.zeros((ye_ref.shape[1] - rows, D_MODEL), BF16)

    w1b[...] = w1_ref[0].astype(BF16)
    w3b[...] = w3_ref[0].astype(BF16)
    fsl = pl.ds(pl.multiple_of(f * tf, tf), tf)
    w2_all[fsl, :] = w2_ref[0].astype(BF16)
    n_chunks = rows // row_chunk
    share = rows // (nf_static * n_chunks)
    for r in range(n_chunks):
        sl = pl.ds(r * row_chunk, row_chunk)
        x = xe[sl, :]
        a = _dot(x, w1b[...])
        bg = _dot(x, w3b[...])
        act_all[sl, fsl] = (a * jax.nn.sigmoid(a) * bg).astype(BF16)
        for u in range(share):
            j = (f * n_chunks + r) * share + u
            start_row((e + 1) * rows + j, j)

    @pl.when(f == nf - 1)
    def _():
        for r in range(n_chunks):
            sl = pl.ds(r * row_chunk, row_chunk)
            ye_ref[0, sl, :] = (_dot(act_all[sl, :], w2_all[...]) * gate[sl, :]).astype(BF16)

        @pl.when(e == ne - 1)
        def _():
            wait_rows()


def _moe(lists, aug, w1, w3, w2, rows, tf=256, row_chunk=512, unroll=8):
    nf = D_EXPERT // tf
    assert rows % (nf * (rows // row_chunk)) == 0 and rows % unroll == 0
    return pl.pallas_call(
        functools.partial(_moe_kernel, rows=rows, row_chunk=row_chunk, unroll=unroll, tf=tf, nf_static=nf),
        grid_spec=pltpu.PrefetchScalarGridSpec(
            num_scalar_prefetch=1,
            grid=(N_EXPERTS, nf),
            in_specs=[pl.BlockSpec(memory_space=pl.ANY),
                      pl.BlockSpec((1, D_MODEL, tf), lambda e, f, ls: (e, 0, f)),
                      pl.BlockSpec((1, D_MODEL, tf), lambda e, f, ls: (e, 0, f)),
                      pl.BlockSpec((1, tf, D_MODEL), lambda e, f, ls: (e, f, 0))],
            out_specs=pl.BlockSpec((1, rows + WINDOW, D_MODEL), lambda e, f, ls: (e, 0, 0)),
            scratch_shapes=[pltpu.VMEM((rows, AUG), F32), pltpu.VMEM((rows, D_MODEL), BF16),
                            pltpu.VMEM((rows, 1), F32),
                            pltpu.VMEM((rows, D_EXPERT), BF16), pltpu.VMEM((D_EXPERT, D_MODEL), BF16),
                            pltpu.VMEM((D_MODEL, tf), BF16), pltpu.VMEM((D_MODEL, tf), BF16),
                            pltpu.SemaphoreType.DMA((1,))],
        ),
        out_shape=jax.ShapeDtypeStruct((N_EXPERTS, rows + WINDOW, D_MODEL), BF16),
        compiler_params=_cparams(("arbitrary", "arbitrary"), VMEM_LIMIT),
        name="moe",
    )(lists, aug, w1, w3, w2)


def _combine_kernel(seg_sm, x1_ref, pos_ref, mod_ref, g_ref, ye_hbm, o_ref, wbuf, sem, *, base, tm):
    i = pl.program_id(0)
    nt = pl.num_programs(0)
    bpt = tm // LANES
    slot = lax.rem(i, 2)
    last_start = ye_hbm.shape[1] - WINDOW
    tn = (((0,), (0,)), ((), ()))

    def start_windows(tile, rnd, buf):
        for e in range(N_EXPERTS):
            start = jnp.minimum(base + (seg_sm[e, tile * bpt] & -ROW_ALIGN) + rnd * WINDOW, last_start)
            pltpu.make_async_copy(ye_hbm.at[e, pl.ds(pl.multiple_of(start, ROW_ALIGN), WINDOW)],
                                  wbuf.at[buf, pl.ds(e * WINDOW, WINDOW)], sem.at[buf]).start()

    def wait_windows(buf):
        pltpu.make_async_copy(wbuf.at[buf], wbuf.at[buf], sem.at[buf]).wait()

    @pl.when(i == 0)
    def _():
        start_windows(i, 0, slot)

    @pl.when(i + 1 < nt)
    def _():
        start_windows(i + 1, 0, 1 - slot)

    most = jnp.int32(0)
    for e in range(N_EXPERTS):
        most = jnp.maximum(most, seg_sm[e, (i + 1) * bpt] - (seg_sm[e, i * bpt] & -ROW_ALIGN))
    n_rounds = jnp.right_shift(most + (WINDOW - 1), WINDOW_SHIFT)

    pos = pos_ref[...]
    valid = pos >= 0
    first = jnp.min(jnp.where(valid, pos, jnp.int32(2 ** 30)), axis=1, keepdims=True)
    rel = pos - (first & -ROW_ALIGN)
    lane = lax.broadcasted_iota(I32, (N_EXPERTS, N_EXPERTS * WINDOW), 1)
    owner = lax.broadcasted_iota(I32, (N_EXPERTS, N_EXPERTS * WINDOW), 0)
    spread = jnp.where(jnp.right_shift(lane, WINDOW_SHIFT) == owner, 1.0, 0.0).astype(BF16)
    wrow = (lax.broadcasted_iota(I32, (1, N_EXPERTS * WINDOW), 1) & (WINDOW - 1)).astype(F32)

    def place(rnd, y):
        q = jnp.where(valid & (rel >= rnd * WINDOW) & (rel < (rnd + 1) * WINDOW), rel - rnd * WINDOW, -1)
        qb = lax.dot_general(q.astype(F32).astype(BF16), spread, tn, preferred_element_type=F32)
        onehot = jnp.where(qb == wrow, 1.0, 0.0).astype(BF16)
        return y + _dot(onehot, wbuf[slot])

    wait_windows(slot)
    y = place(0, jnp.zeros((tm, D_MODEL), F32))

    def extra_round(rnd, y):
        start_windows(i, rnd, slot)
        wait_windows(slot)
        return place(rnd, y)

    y = lax.fori_loop(1, n_rounds, extra_round, y)
    x = x1_ref[...] + mod_ref[0, 5:6, :] * y
    ms = jnp.mean(x * x, axis=-1, keepdims=True)
    o_ref[...] = x * lax.rsqrt(ms + EPS) * g_ref[...]


def _combine_norm(seg, x1, first_row, pos, ye, base, mod, row0, tiles_per_batch, tm, gf):
    n = pos.shape[1]
    tile0 = first_row // tm
    return pl.pallas_call(
        functools.partial(_combine_kernel, base=base, tm=tm),
        grid_spec=pltpu.PrefetchScalarGridSpec(
            num_scalar_prefetch=1,
            grid=(n // tm,),
            in_specs=[pl.BlockSpec((tm, D_MODEL), lambda i, sg: (i + tile0, 0)),
                      pl.BlockSpec((N_EXPERTS, tm), lambda i, sg: (0, i)),
                      pl.BlockSpec((1, N_MOD, D_MODEL),
                                   lambda i, sg: (row0 + i // tiles_per_batch, 0, 0)),
                      pl.BlockSpec((1, D_MODEL), lambda i, sg: (0, 0)),
                      pl.BlockSpec(memory_space=pl.ANY)],
            out_specs=pl.BlockSpec((tm, D_MODEL), lambda i, sg: (i, 0)),
            scratch_shapes=[pltpu.VMEM((2, N_EXPERTS * WINDOW, D_MODEL), BF16),
                            pltpu.SemaphoreType.DMA((2,))],
        ),
        out_shape=jax.ShapeDtypeStruct((n, D_MODEL), F32),
        compiler_params=_cparams(("arbitrary",), VMEM_LIMIT),
        name="combine_norm",
    )(seg, x1, pos, mod, gf, ye)


def kernel(x_prompt, x_sample, state_C, state_n, state_m, c, c_ctx, ada_w, ada_b, norm1_g, norm2_g,
           w_in, b_in, head_g, w_out, router_w, exp_w1, exp_w3, exp_w2, final_g):
    depth = ada_w.shape[0]
    assert depth == 1, "single-layer trunk"
    bc_, tc_, _ = x_prompt.shape
    bl_, tl_, _ = x_sample.shape
    w4 = 4 * MLSTM_WIDTH
    tm = 512

    cvec = jnp.concatenate([c_ctx[None, :], c, jnp.zeros((8 - 1 - bl_, D_MODEL), F32)], axis=0)
    mod = _modulation(cvec, ada_w[0], ada_b[0])

    wi = w_in[0]
    bi = b_in[0]
    wq = wi[:, :w4].astype(BF16)
    bq = bi[None, :w4]
    wg = jnp.pad(wi[:, w4:w4 + 16], ((0, 0), (0, LANES - 16))).astype(BF16)
    bg = jnp.pad(bi[w4:w4 + 16], (0, LANES - 16))[None, :]
    wf = wi[:, w4 + 16:].astype(BF16)
    bfo = bi[None, w4 + 16:]
    colscale = jnp.ones((1, w4), F32).at[:, MLSTM_WIDTH:2 * MLSTM_WIDTH].set(HEAD_DIM ** -0.5)
    g1 = norm1_g[0][None, :]
    g2 = norm2_g[0][None, :]
    wo = w_out[0].astype(BF16)
    wa, wb = wo[:MLSTM_WIDTH], wo[MLSTM_WIDTH:]
    rw = jnp.pad(router_w[0], ((0, 0), (0, LANES - N_EXPERTS)))
    hg3 = head_g[0][:, None, :]

    def spread(v):
        return jnp.broadcast_to(v[..., None, None], v.shape + (1, LANES))

    n_ctx, n_lat = bc_ * tc_, bl_ * tl_

    def mixer(x3, row0, init, grid_w):
        b, t, _ = x3.shape
        x2 = x3.reshape(b * t, D_MODEL)
        tpb = max(t // tm, 1) if row0 else (b * t) // tm + 1
        qkvo, gates, u = _in_proj(x2, mod, row0, tpb, tm, g1, wq, bq, colscale, wg, bg, wf, bfo)
        gate_rows = _gate_sums(gates).reshape(b, t // CHUNK, 2 * N_HEADS, GATE_ROWS, CHUNK)
        c0, n0, m0 = init
        hm, cf, nf, mf = _mlstm(qkvo.reshape(b, t, w4), gate_rows, c0, n0, m0, hg3)
        u3 = u.reshape(b, t, FOURIER_WIDTH)
        fu = _fourier_ctx(u3) if grid_w is None else _fourier_lat(u3, grid_w)
        return (hm.reshape(b * t, MLSTM_WIDTH), fu.reshape(b * t, FOURIER_WIDTH), x2), (cf, nf, mf)

    zero_init = (jnp.zeros((bc_, 2, N_HEADS, HEAD_DIM, HEAD_DIM), F32),
                 jnp.zeros((bc_, 2, N_HEADS, 1, LANES), F32),
                 jnp.zeros((bc_, 2, N_HEADS, 1, LANES), F32))
    lat_init = (state_C[:, 0], state_n[:, 0][..., None, :], spread(state_m[:, 0]))
    ctx_parts, (cf, nf, mf) = mixer(x_prompt, 0, zero_init, None)
    lat_parts, _ = mixer(x_sample, 1, lat_init, GRID_W)
    x1, table, aff_t = _out_proj(ctx_parts, lat_parts, mod, tl_ // tm, tm, g2, wa, wb, rw)

    capc = CAPACITY_FACTOR * n_ctx // N_EXPERTS
    capl = CAPACITY_FACTOR * n_lat // N_EXPERTS
    posc, blkc = _topk(aff_t[:, :n_ctx], capc)
    posl, blkl = _topk(aff_t[:, n_ctx:], capl)
    lists = jnp.concatenate([_compact(posc, blkc, capc), _compact(posl, blkl, capl) + n_ctx], axis=1)
    lists = jnp.concatenate([lists, jnp.zeros((1, capc + capl), I32)], axis=0).reshape(-1)
    ye = _moe(lists, table, exp_w1[0], exp_w3[0], exp_w2[0], capc + capl)
    gf = final_g[None, :]
    tc = 256
    y_prompt = _combine_norm(blkc, x1, 0, posc, ye, 0, mod, 0, n_ctx // tc + 1, tc, gf)
    y_sample = _combine_norm(blkl, x1, n_ctx, posl, ye, capc, mod, 1, tl_ // tc, tc, gf)
    y_prompt = y_prompt.reshape(x_prompt.shape)
    y_sample = y_sample.reshape(x_sample.shape)
    new_c = cf[:, None]
    new_n = nf[:, None, :, :, 0, :]
    new_m = mf[:, None, :, :, 0, 0]
    return (y_prompt, y_sample, new_c, new_n, new_m)
```

```python
import functools
import math

import numpy as np
import jax
import jax.numpy as jnp
from jax import lax
from jax.experimental import pallas as pl
from jax.experimental.pallas import tpu as pltpu

F32 = jnp.float32
BF16 = jnp.bfloat16
I32 = jnp.int32

D_MODEL = 1024
N_HEADS = 4
HEAD_DIM = 128
MLSTM_WIDTH = N_HEADS * HEAD_DIM
FOURIER_WIDTH = 512
N_GROUPS = 4
GROUP_DIM = 128
CHUNK = 128
GRID_W = 64
N_EXPERTS = 16
CAPACITY_FACTOR = 2
D_EXPERT = 2048
N_MOD = 6
EPS = 1e-6
LANES = 128
GATE_ROWS = 8
STATE_PAD = 16
STATE_ROWS = HEAD_DIM + STATE_PAD
ROW_ALIGN = 16
AUG = D_MODEL + LANES
WINDOW_SHIFT = 6
WINDOW = 1 << WINDOW_SHIFT
VMEM_LIMIT = 56 * 1024 * 1024


def _cparams(sem, vmem=None):
    return pltpu.CompilerParams(dimension_semantics=sem, vmem_limit_bytes=vmem)


def _split3(a):
    a1 = a.astype(BF16)
    r1 = a - a1.astype(F32)
    a2 = r1.astype(BF16)
    a3 = (r1 - a2.astype(F32)).astype(BF16)
    return a1, a2, a3


def _dot(a, b):
    return jnp.dot(a, b, preferred_element_type=F32)


def _dot3(a, b):
    a1, a2, _ = _split3(a)
    b1, b2, _ = _split3(b)
    return _dot(a1, b1) + (_dot(a1, b2) + _dot(a2, b1))


def _mod_kernel(c_ref, w_ref, b_ref, o_ref):
    c = c_ref[...]
    s = c * jax.nn.sigmoid(c)
    o_ref[...] = _dot(s.astype(BF16), w_ref[...].astype(BF16)) + b_ref[...]


def _modulation(cvec8, ada_w, ada_b):
    tn = D_MODEL
    out = pl.pallas_call(
        _mod_kernel,
        grid=(N_MOD,),
        in_specs=[pl.BlockSpec((8, D_MODEL), lambda j: (0, 0)),
                  pl.BlockSpec((D_MODEL, tn), lambda j: (0, j)),
                  pl.BlockSpec((1, tn), lambda j: (0, j))],
        out_specs=pl.BlockSpec((8, tn), lambda j: (0, j)),
        out_shape=jax.ShapeDtypeStruct((8, N_MOD * D_MODEL), F32),
        compiler_params=_cparams(("arbitrary",)),
        name="modulation",
    )(cvec8, ada_w, ada_b.reshape(1, -1))
    return out.reshape(8, N_MOD, D_MODEL)


def _mod_index(row0, tiles_per_batch):
    return lambda i: (row0 + i // tiles_per_batch, 0, 0)


def _in_kernel(x_ref, mod_ref, g_ref, wq_ref, bq_ref, sc_ref, wg_ref, bg_ref, wf_ref, bf_ref,
               qkvo_ref, gates_ref, u_ref):
    x = x_ref[...]
    ms = jnp.mean(x * x, axis=-1, keepdims=True)
    y = x * lax.rsqrt(ms + EPS) * g_ref[...]
    h = y * (1.0 + mod_ref[0, 1:2, :]) + mod_ref[0, 0:1, :]
    hb = h.astype(BF16)
    p = (_dot(hb, wq_ref[...]) + bq_ref[...]) * sc_ref[...]
    qkvo_ref[...] = p.astype(BF16)
    gates_ref[...] = _dot(hb, wg_ref[...]) + bg_ref[...]
    u_ref[...] = _dot(hb, wf_ref[...]) + bf_ref[...]


def _in_proj(x2, mod, row0, tiles_per_batch, tm, g1, wq, bq, sc, wg, bg, wf, bfo):
    n = x2.shape[0]
    wq_n = wq.shape[1]
    const = lambda i: (0, 0)
    return pl.pallas_call(
        _in_kernel,
        grid=(n // tm,),
        in_specs=[pl.BlockSpec((tm, D_MODEL), lambda i: (i, 0)),
                  pl.BlockSpec((1, N_MOD, D_MODEL), _mod_index(row0, tiles_per_batch)),
                  pl.BlockSpec((1, D_MODEL), const),
                  pl.BlockSpec((D_MODEL, wq_n), const), pl.BlockSpec((1, wq_n), const),
                  pl.BlockSpec((1, wq_n), const),
                  pl.BlockSpec((D_MODEL, LANES), const), pl.BlockSpec((1, LANES), const),
                  pl.BlockSpec((D_MODEL, FOURIER_WIDTH), const), pl.BlockSpec((1, FOURIER_WIDTH), const)],
        out_specs=[pl.BlockSpec((tm, wq_n), lambda i: (i, 0)),
                   pl.BlockSpec((tm, LANES), lambda i: (i, 0)),
                   pl.BlockSpec((tm, FOURIER_WIDTH), lambda i: (i, 0))],
        out_shape=[jax.ShapeDtypeStruct((n, wq_n), BF16),
                   jax.ShapeDtypeStruct((n, LANES), F32),
                   jax.ShapeDtypeStruct((n, FOURIER_WIDTH), F32)],
        compiler_params=_cparams(("arbitrary",), VMEM_LIMIT),
        name="in_proj",
    )(x2, mod, g1, wq, bq, sc, wg, bg, wf, bfo)


def _gates_kernel(g_ref, out_ref):
    row = lax.broadcasted_iota(I32, (CHUNK, CHUNK), 0)
    col = lax.broadcasted_iota(I32, (CHUNK, CHUNK), 1)
    tril = (col <= row).astype(BF16)
    triu = (col >= row).astype(BF16)
    lane = lax.broadcasted_iota(I32, (CHUNK, LANES), 1)
    out_ref[...] = jnp.zeros(out_ref.shape, F32)
    for c in range(out_ref.shape[0]):
        g = g_ref[pl.ds(c * CHUNK, CHUNK), :]
        ls = jnp.minimum(g, 0.0) - jnp.log(1.0 + jnp.exp(-jnp.abs(g)))
        l1, l2, l3 = _split3(ls)
        pre = _dot(tril, l1) + _dot(tril, l2) + _dot(tril, l3)
        suf = _dot(triu, l1) + _dot(triu, l2) + _dot(triu, l3)
        bct = jnp.where((lane >= 4) & (lane < 8), pre,
                        jnp.where((lane >= 12) & (lane < 16), suf, g)).T
        for head in range(N_HEADS):
            for d in range(2):
                ich = 2 * N_HEADS * d + head
                fch = ich + N_HEADS
                out_ref[c, 2 * head + d, 0:1, :] = bct[ich:ich + 1, :]
                out_ref[c, 2 * head + d, 1:2, :] = bct[fch:fch + 1, :]


def _gate_sums(gates, chunks_per_step=8):
    n = gates.shape[0]
    nc = n // CHUNK
    cps = chunks_per_step
    return pl.pallas_call(
        _gates_kernel,
        grid=(nc // cps,),
        in_specs=[pl.BlockSpec((cps * CHUNK, LANES), lambda i: (i, 0))],
        out_specs=pl.BlockSpec((cps, 2 * N_HEADS, GATE_ROWS, CHUNK), lambda i: (i, 0, 0, 0)),
        out_shape=jax.ShapeDtypeStruct((nc, 2 * N_HEADS, GATE_ROWS, CHUNK), F32),
        compiler_params=_cparams(("arbitrary",)),
        name="gate_sums",
    )(gates)


def _mlstm_kernel(qkvo_ref, g_ref, c0_ref, n0_ref, m0_ref, hg_ref,
                  hm_ref, cf_ref, nf_ref, mf_ref, hf_s, hb_s, c_s, m_s, *, n_chunks):
    c_s[:, :, :HEAD_DIM, :] = c0_ref[0]
    c_s[:, :, HEAD_DIM:, :] = jnp.broadcast_to(n0_ref[0], (2, N_HEADS, STATE_PAD, HEAD_DIM))
    m_s[...] = m0_ref[0]
    src = lax.broadcasted_iota(I32, (CHUNK, CHUNK), 0)
    dst = lax.broadcasted_iota(I32, (CHUNK, CHUNK), 1)
    nt = (((1,), (1,)), ((), ()))

    zero_blk = jnp.zeros((CHUNK, HEAD_DIM), BF16)
    zero_pad = jnp.zeros((STATE_ROWS, HEAD_DIM), BF16)

    def diag2(x_a, x_b, zero):
        return jnp.concatenate([jnp.concatenate([x_a, zero], axis=1),
                                jnp.concatenate([zero, x_b], axis=1)], axis=0)

    def pair_direction(c, pair, d, prev):
        sl = pl.ds(pl.multiple_of(c * CHUNK, CHUNK), CHUNK)
        cols = lambda part: pl.ds((part * N_HEADS + 2 * pair) * HEAD_DIM, 2 * HEAD_DIM)
        q2 = qkvo_ref[0, sl, cols(0)]
        k2 = qkvo_ref[0, sl, cols(1)]
        vt2 = qkvo_ref[0, sl, cols(2)].astype(F32).T
        k_ab = [k2[:, :HEAD_DIM], k2[:, HEAD_DIM:]]
        vt_ab = [vt2[:HEAD_DIM, :], vt2[HEAD_DIM:, :]]
        mask = (src <= dst) if d == 0 else (src >= dst)
        rows = []
        for j in range(2):
            gates = g_ref[0, c, 2 * (2 * pair + j) + d]
            ig_row, b_row = gates[0:1, :], gates[1:2, :]
            gates_t = gates.T
            g = b_row[:, CHUNK - 1:CHUNK] if d == 0 else b_row[:, 0:1]
            w_log = g - b_row + ig_row
            m_loc = jnp.max(w_log, axis=1, keepdims=True)
            w = jnp.exp(w_log - m_loc)
            rows.append((ig_row, b_row, gates_t[:, 0:1], gates_t[:, 1:2], g, m_loc, w))
        c_bf = [prev[j][0].astype(BF16) for j in range(2)]
        lhs_q = jnp.concatenate([diag2(k_ab[0], k_ab[1], zero_blk), diag2(c_bf[0], c_bf[1], zero_pad)], axis=0)
        out_q = lax.dot_general(lhs_q, q2, nt, preferred_element_type=F32)
        s_list, extra = [], []
        for j in range(2):
            ig_row, b_row, ig_col, b_col, g, m_loc, w = rows[j]
            m_prev = prev[j][1]
            dmat = jnp.where(mask, b_row - b_col + ig_col, -jnp.inf)
            inter_log = b_row + m_prev
            m_comb = jnp.maximum(inter_log, jnp.max(dmat, axis=0, keepdims=True))
            s_t = out_q[j * CHUNK:(j + 1) * CHUNK, :] * jnp.exp(dmat - m_comb)
            s_list.append(s_t)
            extra.append((m_comb, jnp.exp(inter_log - m_comb)))
        s2 = jnp.concatenate(s_list, axis=0).astype(BF16)
        num2 = _dot(diag2(vt_ab[0].astype(BF16), vt_ab[1].astype(BF16), zero_blk), s2)
        lhs_k = [jnp.concatenate([vt_ab[j] * rows[j][6], jnp.broadcast_to(rows[j][6], (STATE_PAD, CHUNK))],
                                 axis=0).astype(BF16) for j in range(2)]
        k_rows = jnp.concatenate(k_ab, axis=0)
        c_loc2 = _dot(diag2(lhs_k[0], lhs_k[1], jnp.zeros((STATE_ROWS, CHUNK), BF16)), k_rows)
        res = []
        for j in range(2):
            ig_row, b_row, ig_col, b_col, g, m_loc, w = rows[j]
            c_prev, m_prev = prev[j]
            m_comb, w_inter = extra[j]
            cq = out_q[2 * CHUNK + j * STATE_ROWS:2 * CHUNK + (j + 1) * STATE_ROWS, :]
            num = num2[j * HEAD_DIM:(j + 1) * HEAD_DIM, :] + w_inter * cq[:HEAD_DIM, :]
            den = jnp.sum(s_list[j], axis=0, keepdims=True) + w_inter * cq[HEAD_DIM:HEAD_DIM + 1, :]
            h = num / jnp.maximum(jnp.abs(den), jnp.exp(-m_comb))
            m_new = jnp.maximum(g + m_prev, m_loc)
            a = jnp.exp(g + m_prev - m_new)
            bb = jnp.exp(m_loc - m_new)
            c_loc = c_loc2[j * STATE_ROWS:(j + 1) * STATE_ROWS, :]
            res.append((h, a * c_prev + bb * c_loc, m_new))
        return res

    def scan_body(i, carry):
        units = [(pair, d) for pair in range(N_HEADS // 2) for d in range(2)]
        prev = [[(c_s[d, 2 * pair + j], m_s[d, 2 * pair + j][:, 0:1]) for j in range(2)] for pair, d in units]
        chunk = [i, n_chunks - 1 - i]
        new = [pair_direction(chunk[d], pair, d, prev[u]) for u, (pair, d) in enumerate(units)]
        for (pair, d), res in zip(units, new):
            sl = pl.ds(pl.multiple_of(chunk[d] * CHUNK, CHUNK), CHUNK)
            for j, (h, c_new, m_new) in enumerate(res):
                head = 2 * pair + j
                (hf_s, hb_s)[d][head, :, sl] = h
                c_s[d, head] = c_new
                m_s[d, head] = jnp.broadcast_to(m_new, (1, LANES))
        return carry

    lax.fori_loop(0, n_chunks, scan_body, 0)

    def out_body(c, carry):
        sl = pl.ds(pl.multiple_of(c * CHUNK, CHUNK), CHUNK)
        for head in range(N_HEADS):
            hs = hf_s[head, :, sl] + hb_s[head, :, sl]
            hn = (hs * lax.rsqrt(jnp.mean(hs * hs, axis=0, keepdims=True) + EPS)).T * hg_ref[head]
            o = qkvo_ref[0, sl, pl.ds((3 * N_HEADS + head) * HEAD_DIM, HEAD_DIM)].astype(F32)
            hm_ref[0, sl, pl.ds(head * HEAD_DIM, HEAD_DIM)] = (hn * jax.nn.sigmoid(o)).astype(BF16)
        return carry

    lax.fori_loop(0, n_chunks, out_body, 0)
    cf_ref[0] = c_s[:, :, :HEAD_DIM, :]
    nf_ref[0] = c_s[:, :, HEAD_DIM:HEAD_DIM + 1, :]
    mf_ref[0] = m_s[...]


def _mlstm(qkvo3, gates5, c0, n0, m0, head_g3):
    b, t, w4 = qkvo3.shape
    nc = t // CHUNK
    once = dict(pipeline_mode=pl.Buffered(1))
    st_c = pl.BlockSpec((1, 2, N_HEADS, HEAD_DIM, HEAD_DIM), lambda i: (i, 0, 0, 0, 0))
    st_v = pl.BlockSpec((1, 2, N_HEADS, 1, LANES), lambda i: (i, 0, 0, 0, 0))
    return pl.pallas_call(
        functools.partial(_mlstm_kernel, n_chunks=nc),
        grid=(b,),
        in_specs=[pl.BlockSpec((1, t, w4), lambda i: (i, 0, 0), **once),
                  pl.BlockSpec((1, nc, 2 * N_HEADS, GATE_ROWS, CHUNK), lambda i: (i, 0, 0, 0, 0), **once),
                  st_c, st_v, st_v,
                  pl.BlockSpec((N_HEADS, 1, HEAD_DIM), lambda i: (0, 0, 0))],
        out_specs=[pl.BlockSpec((1, t, MLSTM_WIDTH), lambda i: (i, 0, 0), **once), st_c, st_v, st_v],
        out_shape=[jax.ShapeDtypeStruct((b, t, MLSTM_WIDTH), BF16),
                   jax.ShapeDtypeStruct((b, 2, N_HEADS, HEAD_DIM, HEAD_DIM), F32),
                   jax.ShapeDtypeStruct((b, 2, N_HEADS, 1, LANES), F32),
                   jax.ShapeDtypeStruct((b, 2, N_HEADS, 1, LANES), F32)],
        scratch_shapes=[pltpu.VMEM((N_HEADS, HEAD_DIM, t), F32), pltpu.VMEM((N_HEADS, HEAD_DIM, t), F32),
                        pltpu.VMEM((2, N_HEADS, HEAD_DIM + STATE_PAD, HEAD_DIM), F32),
                        pltpu.VMEM((2, N_HEADS, 1, LANES), F32)],
        compiler_params=_cparams(("arbitrary",), VMEM_LIMIT),
        name="mlstm",
    )(qkvo3, gates5, c0, n0, m0, head_g3)


def _dft_cos_sin(n):
    k = np.arange(n)
    ang = 2.0 * np.pi * ((k[:, None] * k[None, :]) % n) / n
    return np.cos(ang), np.sin(ang)


def _channel_dft(scale):
    cd, sd = _dft_cos_sin(GROUP_DIM)
    eye = np.eye(N_GROUPS)
    return (jnp.asarray(np.kron(eye, cd) * scale, F32), jnp.asarray(np.kron(eye, -sd) * scale, F32))


def _fourier_ctx_kernel(u_ref, bdc_ref, bds_ref, ct_ref, st_ref, o_ref):
    u = u_ref[0]
    a = _dot3(u, bdc_ref[...])
    b = _dot3(u, bds_ref[...])
    o_ref[0] = (_dot3(ct_ref[...], a) + _dot3(st_ref[...], b)).astype(BF16)


def _fourier_ctx(u3):
    b, t, w = u3.shape
    bdc, bds = _channel_dft(1.0 / math.sqrt(t * GROUP_DIM))
    ct, st = _dft_cos_sin(t)
    const = lambda i: (0, 0)
    return pl.pallas_call(
        _fourier_ctx_kernel,
        grid=(b,),
        in_specs=[pl.BlockSpec((1, t, w), lambda i: (i, 0, 0)),
                  pl.BlockSpec((w, w), const), pl.BlockSpec((w, w), const),
                  pl.BlockSpec((t, t), const), pl.BlockSpec((t, t), const)],
        out_specs=pl.BlockSpec((1, t, w), lambda i: (i, 0, 0)),
        out_shape=jax.ShapeDtypeStruct((b, t, w), BF16),
        compiler_params=_cparams(("arbitrary",)),
        name="fourier_ctx",
    )(u3, bdc, bds, jnp.asarray(ct, F32), jnp.asarray(st, F32))


def _fourier_chan_kernel(u_ref, bdc_ref, bds_ref, a_ref, b_ref):
    u = u_ref[0].astype(BF16)
    a_ref[0] = _dot(u, bdc_ref[...]).astype(BF16)
    b_ref[0] = _dot(u, bds_ref[...]).astype(BF16)


def _fourier_pos_kernel(cre_ref, sre_ref, cct_ref, sct_ref, a_ref, b_ref, o_ref, cp_s, sp_s):
    @pl.when(pl.program_id(1) == 0)
    def _():
        gw = cct_ref.shape[0]
        for q in range(cre_ref.shape[0]):
            cr, sr = cre_ref[q], sre_ref[q]
            cc, sc = cct_ref[...], sct_ref[...]
            cp_s[q * gw:(q + 1) * gw, :] = (cr * cc - sr * sc).astype(BF16)
            sp_s[q * gw:(q + 1) * gw, :] = (sr * cc + cr * sc).astype(BF16)

    o_ref[0] = (_dot(cp_s[...], a_ref[0]) + _dot(sp_s[...], b_ref[0])).astype(BF16)


def _fourier_lat(u3, grid_w):
    b, t, w = u3.shape
    rows = t // grid_w
    bdc, bds = _channel_dft(1.0 / math.sqrt(t * GROUP_DIM))
    tm = 512
    const = lambda i, j: (0, 0)
    a, bm = pl.pallas_call(
        _fourier_chan_kernel,
        grid=(b, t // tm),
        in_specs=[pl.BlockSpec((1, tm, w), lambda i, j: (i, j, 0)),
                  pl.BlockSpec((w, w), const), pl.BlockSpec((w, w), const)],
        out_specs=[pl.BlockSpec((1, tm, w), lambda i, j: (i, j, 0))] * 2,
        out_shape=[jax.ShapeDtypeStruct((b, t, w), BF16)] * 2,
        compiler_params=_cparams(("arbitrary", "arbitrary")),
        name="fourier_chan",
    )(u3, bdc.astype(BF16), bds.astype(BF16))
    cr, sr = _dft_cos_sin(rows)
    cc, sc = _dft_cos_sin(grid_w)
    cre = jnp.asarray(np.repeat(cr, grid_w, axis=1)[:, None, :], F32)
    sre = jnp.asarray(np.repeat(sr, grid_w, axis=1)[:, None, :], F32)
    cct = jnp.asarray(np.tile(cc, (1, rows)), F32)
    sct = jnp.asarray(np.tile(sc, (1, rows)), F32)
    rpt = tm // grid_w
    return pl.pallas_call(
        _fourier_pos_kernel,
        grid=(t // tm, b),
        in_specs=[pl.BlockSpec((rpt, 1, t), lambda j, i: (j, 0, 0)),
                  pl.BlockSpec((rpt, 1, t), lambda j, i: (j, 0, 0)),
                  pl.BlockSpec((grid_w, t), lambda j, i: (0, 0)),
                  pl.BlockSpec((grid_w, t), lambda j, i: (0, 0)),
                  pl.BlockSpec((1, t, w), lambda j, i: (i, 0, 0)),
                  pl.BlockSpec((1, t, w), lambda j, i: (i, 0, 0))],
        out_specs=pl.BlockSpec((1, tm, w), lambda j, i: (i, j, 0)),
        out_shape=jax.ShapeDtypeStruct((b, t, w), BF16),
        scratch_shapes=[pltpu.VMEM((tm, t), BF16), pltpu.VMEM((tm, t), BF16)],
        compiler_params=_cparams(("arbitrary", "arbitrary"), VMEM_LIMIT),
        name="fourier_pos",
    )(cre, sre, cct, sct, a, bm)


def _out_kernel(hmc_ref, fuc_ref, xc_ref, hml_ref, ful_ref, xl_ref, mod_ref, g_ref, wa_ref, wb_ref, rw_ref,
                x1_ref, aug_ref, afft_ref, *, ctx_tiles):
    def tile(hm_ref, fu_ref, x_ref):
        a = _dot(hm_ref[...], wa_ref[...]) + _dot(fu_ref[...], wb_ref[...])
        x1 = x_ref[...] + mod_ref[0, 2:3, :] * a
        x1_ref[...] = x1
        ms = jnp.mean(x1 * x1, axis=-1, keepdims=True)
        y = x1 * lax.rsqrt(ms + EPS) * g_ref[...]
        h2 = y * (1.0 + mod_ref[0, 4:5, :]) + mod_ref[0, 3:4, :]
        h_hi, h_lo, _ = _split3(h2)
        w_hi, w_lo, _ = _split3(rw_ref[...])
        both = _dot(h_hi, jnp.concatenate([w_hi, w_lo], axis=1))
        logits = both[:, :LANES] + (both[:, LANES:] + _dot(h_lo, w_hi))
        lane = lax.broadcasted_iota(I32, logits.shape, 1)
        valid = lane < N_EXPERTS
        lg = jnp.where(valid, logits, -1e30)
        ex = jnp.where(valid, jnp.exp(lg - jnp.max(lg, axis=1, keepdims=True)), 0.0)
        aff = ex / jnp.sum(ex, axis=1, keepdims=True)
        aug_ref[:, :D_MODEL] = h2
        aug_ref[:, D_MODEL:] = aff
        afft_ref[...] = aff.T[:N_EXPERTS, :]

    @pl.when(pl.program_id(0) < ctx_tiles)
    def _():
        tile(hmc_ref, fuc_ref, xc_ref)

    @pl.when(pl.program_id(0) >= ctx_tiles)
    def _():
        tile(hml_ref, ful_ref, xl_ref)


def _out_proj(ctx, lat, mod, lat_tiles_per_batch, tm, g2, wa, wb, rw):
    n_c, n_l = ctx[2].shape[0], lat[2].shape[0]
    n = n_c + n_l
    ct = n_c // tm
    const = lambda i: (0, 0)
    ctx_tile = lambda i: (jnp.minimum(i, ct - 1), 0)
    lat_tile = lambda i: (jnp.maximum(i - ct, 0), 0)
    widths = (MLSTM_WIDTH, FOURIER_WIDTH, D_MODEL)
    return pl.pallas_call(
        functools.partial(_out_kernel, ctx_tiles=ct),
        grid=(n // tm,),
        in_specs=[pl.BlockSpec((tm, w), ctx_tile) for w in widths]
                 + [pl.BlockSpec((tm, w), lat_tile) for w in widths]
                 + [pl.BlockSpec((1, N_MOD, D_MODEL),
                                 lambda i: (jnp.where(i < ct, 0, 1 + (i - ct) // lat_tiles_per_batch), 0, 0)),
                    pl.BlockSpec((1, D_MODEL), const),
                    pl.BlockSpec((MLSTM_WIDTH, D_MODEL), const),
                    pl.BlockSpec((FOURIER_WIDTH, D_MODEL), const),
                    pl.BlockSpec((D_MODEL, LANES), const)],
        out_specs=[pl.BlockSpec((tm, D_MODEL), lambda i: (i, 0)),
                   pl.BlockSpec((tm, AUG), lambda i: (i, 0)),
                   pl.BlockSpec((N_EXPERTS, tm), lambda i: (0, i))],
        out_shape=[jax.ShapeDtypeStruct((n, D_MODEL), F32),
                   jax.ShapeDtypeStruct((n, AUG), F32),
                   jax.ShapeDtypeStruct((N_EXPERTS, n), F32)],
        compiler_params=_cparams(("arbitrary",), VMEM_LIMIT),
        name="out_proj",
    )(*ctx, *lat, mod, g2, wa, wb, rw)


def _topk_kernel(aff_ref, pos_ref, blk_ref, *, n, cap):
    nb = n // LANES
    aff = aff_ref[...]

    def enough(t):
        return jnp.sum(jnp.where(aff >= t, 1.0, 0.0), axis=1, keepdims=True) >= cap

    def pow2(k):
        return lax.bitcast_convert_type(jnp.left_shift(127 - k, 23), F32)

    def exp_search(_, c):
        lo, hi = c
        mid = jnp.right_shift(lo + hi, 1)
        ok = enough(pow2(jnp.minimum(mid, 126)))
        return jnp.where(ok, lo, mid + 1), jnp.where(ok, mid, hi)

    kz = jnp.zeros((N_EXPERTS, 1), I32)
    kstar, _ = lax.fori_loop(0, 7, exp_search, (kz, kz + 127))
    found = kstar < 127
    p = pow2(jnp.minimum(kstar, 126))
    t_lo0 = jnp.where(found, p, 0.0)
    t_hi0 = jnp.where(found, 2.0 * p, p)

    def bisect(_, c):
        t_lo, t_hi = c
        mid = 0.5 * (t_lo + t_hi)
        ok = enough(mid)
        return jnp.where(ok, mid, t_lo), jnp.where(ok, t_hi, mid)

    t_lo, t_hi = lax.fori_loop(0, 40, bisect, (t_lo0, t_hi0))
    n_gt = jnp.sum(jnp.where(aff >= t_hi, 1.0, 0.0), axis=1, keepdims=True)
    need = cap - n_gt
    row = lax.broadcasted_iota(I32, (LANES, LANES), 0)
    col = lax.broadcasted_iota(I32, (LANES, LANES), 1)
    upper = (row <= col).astype(BF16)
    blane = lax.broadcasted_iota(I32, blk_ref.shape, 1)

    def block(b, carry):
        eq_off, sel_off = carry
        sl = pl.ds(pl.multiple_of(b * LANES, LANES), LANES)
        ab = aff_ref[:, sl]
        gt = ab >= t_hi
        eq = (ab >= t_lo) & (ab < t_hi)
        eqf = jnp.where(eq, 1.0, 0.0)
        eq_rank = _dot(eqf.astype(BF16), upper) + eq_off - eqf
        sel = gt | (eq & (eq_rank < need))
        self_ = jnp.where(sel, 1.0, 0.0)
        cum = _dot(self_.astype(BF16), upper) + sel_off
        pos_ref[:, sl] = jnp.where(sel, cum - 1.0, -1.0).astype(I32)
        blk_ref[...] = jnp.where(blane == b, sel_off.astype(I32), blk_ref[...])
        return (eq_off + jnp.sum(eqf, axis=1, keepdims=True),
                sel_off + jnp.sum(self_, axis=1, keepdims=True))

    blk_ref[...] = jnp.full(blk_ref.shape, cap, I32)
    zero = jnp.zeros((N_EXPERTS, 1), F32)
    lax.fori_loop(0, nb, block, (zero, zero))


def _topk(aff_t, cap):
    n = aff_t.shape[1]
    return pl.pallas_call(
        functools.partial(_topk_kernel, n=n, cap=cap),
        grid=(1,),
        in_specs=[pl.BlockSpec((N_EXPERTS, n), lambda i: (0, 0))],
        out_specs=[pl.BlockSpec((N_EXPERTS, n), lambda i: (0, 0)),
                   pl.BlockSpec((N_EXPERTS, 2 * LANES), lambda i: (0, 0))],
        out_shape=[jax.ShapeDtypeStruct((N_EXPERTS, n), I32),
                   jax.ShapeDtypeStruct((N_EXPERTS, 2 * LANES), I32)],
        compiler_params=_cparams(("arbitrary",)),
        name="topk",
    )(aff_t)


def _compact_kernel(blk_sm, pos_ref, out_ref, *, nb):
    e = pl.program_id(0)

    slot0 = lax.broadcasted_iota(I32, (LANES, LANES), 0)
    tok = lax.broadcasted_iota(I32, (LANES, LANES), 1)

    def chunk(jc, carry):
        first, last = carry
        j0 = jc * LANES
        first = lax.while_loop(lambda b: blk_sm[e, b + 1] <= j0, lambda b: b + 1, first)
        last = lax.while_loop(lambda b: (b < nb) & (blk_sm[e, b] < j0 + LANES), lambda b: b + 1, last)
        slot = slot0 + j0

        def body(b, acc):
            prow = pos_ref[0, :, pl.ds(pl.multiple_of(b * LANES, LANES), LANES)]
            return acc + jnp.where(prow == slot, (tok + b * LANES).astype(F32), 0.0)

        acc = lax.fori_loop(first, last, body, jnp.zeros((LANES, LANES), F32))
        out_ref[0, jc] = jnp.sum(acc.T, axis=0, keepdims=True).astype(I32)
        return first, last

    lax.fori_loop(0, out_ref.shape[1], chunk, (jnp.int32(0), jnp.int32(0)))


def _compact(pos, blk, cap):
    n = pos.shape[1]
    ncj = cap // LANES
    out = pl.pallas_call(
        functools.partial(_compact_kernel, nb=n // LANES),
        grid_spec=pltpu.PrefetchScalarGridSpec(
            num_scalar_prefetch=1,
            grid=(N_EXPERTS,),
            in_specs=[pl.BlockSpec((1, 1, n), lambda e, blk: (e, 0, 0))],
            out_specs=pl.BlockSpec((1, ncj, 1, LANES), lambda e, blk: (e, 0, 0, 0)),
        ),
        out_shape=jax.ShapeDtypeStruct((N_EXPERTS, ncj, 1, LANES), I32),
        compiler_params=_cparams(("arbitrary",)),
        name="compact",
    )(blk, pos.reshape(N_EXPERTS, 1, n))
    return out.reshape(N_EXPERTS, cap)


def _moe_kernel(lists_sm, aug, w1_ref, w3_ref, w2_ref, ye_ref,
                xbuf, xe, gate, act_all, w2_all, w1b, w3b, sem, *, rows, row_chunk, unroll, tf, nf_static):
    e = pl.program_id(0)
    f = pl.program_id(1)
    ne = pl.num_programs(0)
    nf = pl.num_programs(1)

    def start_row(list_pos, j):
        pltpu.make_async_copy(aug.at[pl.ds(lists_sm[list_pos], 1)], xbuf.at[pl.ds(j, 1)], sem.at[0]).start()

    def wait_rows():
        pltpu.make_async_copy(xbuf, xbuf, sem.at[0]).wait()

    @pl.when((e == 0) & (f == 0))
    def _():
        def body(g, carry):
            for u in range(unroll):
                start_row(g * unroll + u, g * unroll + u)
            return carry

        lax.fori_loop(0, rows // unroll, body, 0)

    @pl.when(f == 0)
    def _():
        wait_rows()
        xe[...] = xbuf[:, :D_MODEL].astype(BF16)
        lane = lax.broadcasted_iota(I32, (rows, LANES), 1)
        gate[...] = jnp.sum(jnp.where(lane == e, xbuf[:, D_MODEL:], 0.0), axis=1, keepdims=True)
        ye_ref[0, rows:, :] = jnp.zeros((ye_ref.shape[1] - rows, D_MODEL), BF16)

    w1b[...] = w1_ref[0].astype(BF16)
    w3b[...] = w3_ref[0].astype(BF16)
    fsl = pl.ds(pl.multiple_of(f * tf, tf), tf)
    w2_all[fsl, :] = w2_ref[0].astype(BF16)
    n_chunks = rows // row_chunk
    up_share = rows // (2 * nf_static * n_chunks)
    down_share = rows // (2 * n_chunks)
    nxt = (e + 1) * rows
    for r in range(n_chunks):
        sl = pl.ds(r * row_chunk, row_chunk)
        x = xe[sl, :]
        a = _dot(x, w1b[...])
        bg = _dot(x, w3b[...])
        act_all[sl, fsl] = (a * jax.nn.sigmoid(a) * bg).astype(BF16)
        for u in range(up_share):
            j = (f * n_chunks + r) * up_share + u
            start_row(nxt + j, j)

    @pl.when(f == nf - 1)
    def _():
        for r in range(n_chunks):
            sl = pl.ds(r * row_chunk, row_chunk)
            ye_ref[0, sl, :] = (_dot(act_all[sl, :], w2_all[...]) * gate[sl, :]).astype(BF16)
            for u in range(down_share):
                j = rows // 2 + r * down_share + u
                start_row(nxt + j, j)

        @pl.when(e == ne - 1)
        def _():
            wait_rows()


def _moe(lists, aug, w1, w3, w2, rows, tf=256, row_chunk=512, unroll=8):
    nf = D_EXPERT // tf
    assert rows % (2 * nf * (rows // row_chunk)) == 0 and rows % unroll == 0
    return pl.pallas_call(
        functools.partial(_moe_kernel, rows=rows, row_chunk=row_chunk, unroll=unroll, tf=tf, nf_static=nf),
        grid_spec=pltpu.PrefetchScalarGridSpec(
            num_scalar_prefetch=1,
            grid=(N_EXPERTS, nf),
            in_specs=[pl.BlockSpec(memory_space=pl.ANY),
                      pl.BlockSpec((1, D_MODEL, tf), lambda e, f, ls: (e, 0, f)),
                      pl.BlockSpec((1, D_MODEL, tf), lambda e, f, ls: (e, 0, f)),
                      pl.BlockSpec((1, tf, D_MODEL), lambda e, f, ls: (e, f, 0))],
            out_specs=pl.BlockSpec((1, rows + WINDOW, D_MODEL), lambda e, f, ls: (e, 0, 0)),
            scratch_shapes=[pltpu.VMEM((rows, AUG), F32), pltpu.VMEM((rows, D_MODEL), BF16),
                            pltpu.VMEM((rows, 1), F32),
                            pltpu.VMEM((rows, D_EXPERT), BF16), pltpu.VMEM((D_EXPERT, D_MODEL), BF16),
                            pltpu.VMEM((D_MODEL, tf), BF16), pltpu.VMEM((D_MODEL, tf), BF16),
                            pltpu.SemaphoreType.DMA((1,))],
        ),
        out_shape=jax.ShapeDtypeStruct((N_EXPERTS, rows + WINDOW, D_MODEL), BF16),
        compiler_params=_cparams(("arbitrary", "arbitrary"), VMEM_LIMIT),
        name="moe",
    )(lists, aug, w1, w3, w2)


def _combine_kernel(seg_sm, x1_ref, pos_ref, mod_ref, g_ref, ye_hbm, o_ref, wbuf, y_s, sem, *, base, tm):
    i = pl.program_id(0)
    nt = pl.num_programs(0)
    bpt = tm // LANES
    slot = lax.rem(i, 2)
    last_start = ye_hbm.shape[1] - WINDOW
    tn = (((0,), (0,)), ((), ()))

    def start_windows(tile, rnd, buf):
        for e in range(N_EXPERTS):
            start = jnp.minimum(base + (seg_sm[e, tile * bpt] & -ROW_ALIGN) + rnd * WINDOW, last_start)
            pltpu.make_async_copy(ye_hbm.at[e, pl.ds(pl.multiple_of(start, ROW_ALIGN), WINDOW)],
                                  wbuf.at[buf, pl.ds(e * WINDOW, WINDOW)], sem.at[buf]).start()

    def wait_windows(buf):
        pltpu.make_async_copy(wbuf.at[buf], wbuf.at[buf], sem.at[buf]).wait()

    @pl.when(i == 0)
    def _():
        start_windows(i, 0, slot)

    most = jnp.int32(0)
    for e in range(N_EXPERTS):
        most = jnp.maximum(most, seg_sm[e, (i + 1) * bpt] - (seg_sm[e, i * bpt] & -ROW_ALIGN))
    n_rounds = jnp.right_shift(most + (WINDOW - 1), WINDOW_SHIFT)

    pos = pos_ref[...]
    valid = pos >= 0
    first = jnp.min(jnp.where(valid, pos, jnp.int32(2 ** 30)), axis=1, keepdims=True)
    rel = pos - (first & -ROW_ALIGN)
    lane = lax.broadcasted_iota(I32, (N_EXPERTS, N_EXPERTS * WINDOW), 1)
    owner = lax.broadcasted_iota(I32, (N_EXPERTS, N_EXPERTS * WINDOW), 0)
    spread = jnp.where(jnp.right_shift(lane, WINDOW_SHIFT) == owner, 1.0, 0.0).astype(BF16)
    wrow = (lax.broadcasted_iota(I32, (1, N_EXPERTS * WINDOW), 1) & (WINDOW - 1)).astype(F32)

    def place(rnd, first_round):
        q = jnp.where(valid & (rel >= rnd * WINDOW) & (rel < (rnd + 1) * WINDOW), rel - rnd * WINDOW, -1)
        q = q.astype(F32).astype(BF16)
        for part in range(tm // LANES):
            tok = pl.ds(part * LANES, LANES)
            qb = lax.dot_general(q[:, part * LANES:(part + 1) * LANES], spread, tn, preferred_element_type=F32)
            onehot = jnp.where(qb == wrow, 1.0, 0.0).astype(BF16)
            rows_y = _dot(onehot, wbuf[slot])
            if first_round:
                y_s[tok, :] = rows_y
            else:
                y_s[tok, :] += rows_y

    wait_windows(slot)
    start_windows(jnp.minimum(i + 1, nt - 1), 0, 1 - slot)
    place(0, True)

    def extra_round(rnd, carry):
        start_windows(i, rnd, slot)
        wait_windows(slot)
        place(rnd, False)
        return carry

    lax.fori_loop(1, n_rounds, extra_round, 0)
    x = x1_ref[...] + mod_ref[0, 5:6, :] * y_s[...]
    ms = jnp.mean(x * x, axis=-1, keepdims=True)
    o_ref[...] = x * lax.rsqrt(ms + EPS) * g_ref[...]

    @pl.when(i == nt - 1)
    def _():
        wait_windows(1 - slot)


def _combine_norm(seg, x1, first_row, pos, ye, base, mod, row0, tiles_per_batch, tm, gf):
    n = pos.shape[1]
    tile0 = first_row // tm
    return pl.pallas_call(
        functools.partial(_combine_kernel, base=base, tm=tm),
        grid_spec=pltpu.PrefetchScalarGridSpec(
            num_scalar_prefetch=1,
            grid=(n // tm,),
            in_specs=[pl.BlockSpec((tm, D_MODEL), lambda i, sg: (i + tile0, 0)),
                      pl.BlockSpec((N_EXPERTS, tm), lambda i, sg: (0, i)),
                      pl.BlockSpec((1, N_MOD, D_MODEL),
                                   lambda i, sg: (row0 + i // tiles_per_batch, 0, 0)),
                      pl.BlockSpec((1, D_MODEL), lambda i, sg: (0, 0)),
                      pl.BlockSpec(memory_space=pl.ANY)],
            out_specs=pl.BlockSpec((tm, D_MODEL), lambda i, sg: (i, 0)),
            scratch_shapes=[pltpu.VMEM((2, N_EXPERTS * WINDOW, D_MODEL), BF16),
                            pltpu.VMEM((tm, D_MODEL), F32),
                            pltpu.SemaphoreType.DMA((2,))],
        ),
        out_shape=jax.ShapeDtypeStruct((n, D_MODEL), F32),
        compiler_params=_cparams(("arbitrary",), VMEM_LIMIT),
        name="combine_norm",
    )(seg, x1, pos, mod, gf, ye)


def kernel(x_prompt, x_sample, state_C, state_n, state_m, c, c_ctx, ada_w, ada_b, norm1_g, norm2_g,
           w_in, b_in, head_g, w_out, router_w, exp_w1, exp_w3, exp_w2, final_g):
    depth = ada_w.shape[0]
    assert depth == 1, "single-layer trunk"
    bc_, tc_, _ = x_prompt.shape
    bl_, tl_, _ = x_sample.shape
    w4 = 4 * MLSTM_WIDTH
    tm = 512

    cvec = jnp.concatenate([c_ctx[None, :], c, jnp.zeros((8 - 1 - bl_, D_MODEL), F32)], axis=0)
    mod = _modulation(cvec, ada_w[0], ada_b[0])

    wi = w_in[0]
    bi = b_in[0]
    wq = wi[:, :w4].astype(BF16)
    bq = bi[None, :w4]
    wg = jnp.pad(wi[:, w4:w4 + 16], ((0, 0), (0, LANES - 16))).astype(BF16)
    bg = jnp.pad(bi[w4:w4 + 16], (0, LANES - 16))[None, :]
    wf = wi[:, w4 + 16:].astype(BF16)
    bfo = bi[None, w4 + 16:]
    colscale = jnp.ones((1, w4), F32).at[:, MLSTM_WIDTH:2 * MLSTM_WIDTH].set(HEAD_DIM ** -0.5)
    g1 = norm1_g[0][None, :]
    g2 = norm2_g[0][None, :]
    wo = w_out[0].astype(BF16)
    wa, wb = wo[:MLSTM_WIDTH], wo[MLSTM_WIDTH:]
    rw = jnp.pad(router_w[0], ((0, 0), (0, LANES - N_EXPERTS)))
    hg3 = head_g[0][:, None, :]

    def spread(v):
        return jnp.broadcast_to(v[..., None, None], v.shape + (1, LANES))

    n_ctx, n_lat = bc_ * tc_, bl_ * tl_

    def mixer(x3, row0, init, grid_w):
        b, t, _ = x3.shape
        x2 = x3.reshape(b * t, D_MODEL)
        tpb = max(t // tm, 1) if row0 else (b * t) // tm + 1
        qkvo, gates, u = _in_proj(x2, mod, row0, tpb, tm, g1, wq, bq, colscale, wg, bg, wf, bfo)
        gate_rows = _gate_sums(gates).reshape(b, t // CHUNK, 2 * N_HEADS, GATE_ROWS, CHUNK)
        c0, n0, m0 = init
        hm, cf, nf, mf = _mlstm(qkvo.reshape(b, t, w4), gate_rows, c0, n0, m0, hg3)
        u3 = u.reshape(b, t, FOURIER_WIDTH)
        fu = _fourier_ctx(u3) if grid_w is None else _fourier_lat(u3, grid_w)
        return (hm.reshape(b * t, MLSTM_WIDTH), fu.reshape(b * t, FOURIER_WIDTH), x2), (cf, nf, mf)

    zero_init = (jnp.zeros((bc_, 2, N_HEADS, HEAD_DIM, HEAD_DIM), F32),
                 jnp.zeros((bc_, 2, N_HEADS, 1, LANES), F32),
                 jnp.zeros((bc_, 2, N_HEADS, 1, LANES), F32))
    lat_init = (state_C[:, 0], state_n[:, 0][..., None, :], spread(state_m[:, 0]))
    ctx_parts, (cf, nf, mf) = mixer(x_prompt, 0, zero_init, None)
    lat_parts, _ = mixer(x_sample, 1, lat_init, GRID_W)
    tmo = 256
    x1, table, aff_t = _out_proj(ctx_parts, lat_parts, mod, tl_ // tmo, tmo, g2, wa, wb, rw)

    capc = CAPACITY_FACTOR * n_ctx // N_EXPERTS
    capl = CAPACITY_FACTOR * n_lat // N_EXPERTS
    posc, blkc = _topk(aff_t[:, :n_ctx], capc)
    posl, blkl = _topk(aff_t[:, n_ctx:], capl)
    lists = jnp.concatenate([_compact(posc, blkc, capc), _compact(posl, blkl, capl) + n_ctx], axis=1)
    lists = jnp.concatenate([lists, jnp.zeros((1, capc + capl), I32)], axis=0).reshape(-1)
    ye = _moe(lists, table, exp_w1[0], exp_w3[0], exp_w2[0], capc + capl)
    gf = final_g[None, :]
    tc = 256
    y_prompt = _combine_norm(blkc, x1, 0, posc, ye, 0, mod, 0, n_ctx // tc + 1, tc, gf)
    y_sample = _combine_norm(blkl, x1, n_ctx, posl, ye, capc, mod, 1, tl_ // tc, tc, gf)
    y_prompt = y_prompt.reshape(x_prompt.shape)
    y_sample = y_sample.reshape(x_sample.shape)
    new_c = cf[:, None]
    new_n = nf[:, None, :, :, 0, :]
    new_m = mf[:, None, :, :, 0, 0]
    return (y_prompt, y_sample, new_c, new_n, new_m)
```

```python
import functools
import math

import numpy as np
import jax
import jax.numpy as jnp
from jax import lax
from jax.experimental import pallas as pl
from jax.experimental.pallas import tpu as pltpu

F32 = jnp.float32
BF16 = jnp.bfloat16
I32 = jnp.int32

D_MODEL = 1024
N_HEADS = 4
HEAD_DIM = 128
MLSTM_WIDTH = N_HEADS * HEAD_DIM
FOURIER_WIDTH = 512
N_GROUPS = 4
GROUP_DIM = 128
CHUNK = 128
GRID_W = 64
N_EXPERTS = 16
CAPACITY_FACTOR = 2
D_EXPERT = 2048
N_MOD = 6
EPS = 1e-6
LANES = 128
GATE_ROWS = 8
STATE_PAD = 16
STATE_ROWS = HEAD_DIM + STATE_PAD
ROW_ALIGN = 16
AUG = D_MODEL + LANES
RANGE_HALF = 64
WINDOW_SHIFT = 6
WINDOW = 1 << WINDOW_SHIFT
VMEM_LIMIT = 56 * 1024 * 1024


def _cparams(sem, vmem=None):
    return pltpu.CompilerParams(dimension_semantics=sem, vmem_limit_bytes=vmem)


def _split3(a):
    a1 = a.astype(BF16)
    r1 = a - a1.astype(F32)
    a2 = r1.astype(BF16)
    a3 = (r1 - a2.astype(F32)).astype(BF16)
    return a1, a2, a3


def _dot(a, b):
    return jnp.dot(a, b, preferred_element_type=F32)


def _dot3(a, b):
    a1, a2, _ = _split3(a)
    b1, b2, _ = _split3(b)
    return _dot(a1, b1) + (_dot(a1, b2) + _dot(a2, b1))


def _mod_kernel(c_ref, w_ref, b_ref, o_ref):
    c = c_ref[...]
    s = c * jax.nn.sigmoid(c)
    o_ref[...] = _dot(s.astype(BF16), w_ref[...].astype(BF16)) + b_ref[...]


def _modulation(cvec8, ada_w, ada_b):
    tn = D_MODEL
    out = pl.pallas_call(
        _mod_kernel,
        grid=(N_MOD,),
        in_specs=[pl.BlockSpec((8, D_MODEL), lambda j: (0, 0)),
                  pl.BlockSpec((D_MODEL, tn), lambda j: (0, j)),
                  pl.BlockSpec((1, tn), lambda j: (0, j))],
        out_specs=pl.BlockSpec((8, tn), lambda j: (0, j)),
        out_shape=jax.ShapeDtypeStruct((8, N_MOD * D_MODEL), F32),
        compiler_params=_cparams(("arbitrary",)),
        name="modulation",
    )(cvec8, ada_w, ada_b.reshape(1, -1))
    return out.reshape(8, N_MOD, D_MODEL)


def _mod_index(row0, tiles_per_batch):
    return lambda i: (row0 + i // tiles_per_batch, 0, 0)


def _in_kernel(x_ref, mod_ref, g_ref, wq_ref, bq_ref, sc_ref, wg_ref, bg_ref, wf_ref, bf_ref,
               qkvo_ref, gates_ref, u_ref):
    x = x_ref[...]
    ms = jnp.mean(x * x, axis=-1, keepdims=True)
    y = x * lax.rsqrt(ms + EPS) * g_ref[...]
    h = y * (1.0 + mod_ref[0, 1:2, :]) + mod_ref[0, 0:1, :]
    hb = h.astype(BF16)
    p = (_dot(hb, wq_ref[...]) + bq_ref[...]) * sc_ref[...]
    qkvo_ref[...] = p.astype(BF16)
    gates_ref[...] = _dot(hb, wg_ref[...]) + bg_ref[...]
    u_ref[...] = _dot(hb, wf_ref[...]) + bf_ref[...]


def _in_proj(x2, mod, row0, tiles_per_batch, tm, g1, wq, bq, sc, wg, bg, wf, bfo):
    n = x2.shape[0]
    wq_n = wq.shape[1]
    const = lambda i: (0, 0)
    return pl.pallas_call(
        _in_kernel,
        grid=(n // tm,),
        in_specs=[pl.BlockSpec((tm, D_MODEL), lambda i: (i, 0)),
                  pl.BlockSpec((1, N_MOD, D_MODEL), _mod_index(row0, tiles_per_batch)),
                  pl.BlockSpec((1, D_MODEL), const),
                  pl.BlockSpec((D_MODEL, wq_n), const), pl.BlockSpec((1, wq_n), const),
                  pl.BlockSpec((1, wq_n), const),
                  pl.BlockSpec((D_MODEL, LANES), const), pl.BlockSpec((1, LANES), const),
                  pl.BlockSpec((D_MODEL, FOURIER_WIDTH), const), pl.BlockSpec((1, FOURIER_WIDTH), const)],
        out_specs=[pl.BlockSpec((tm, wq_n), lambda i: (i, 0)),
                   pl.BlockSpec((tm, LANES), lambda i: (i, 0)),
                   pl.BlockSpec((tm, FOURIER_WIDTH), lambda i: (i, 0))],
        out_shape=[jax.ShapeDtypeStruct((n, wq_n), BF16),
                   jax.ShapeDtypeStruct((n, LANES), F32),
                   jax.ShapeDtypeStruct((n, FOURIER_WIDTH), F32)],
        compiler_params=_cparams(("arbitrary",), VMEM_LIMIT),
        name="in_proj",
    )(x2, mod, g1, wq, bq, sc, wg, bg, wf, bfo)


def _gates_kernel(g_ref, out_ref):
    row = lax.broadcasted_iota(I32, (CHUNK, CHUNK), 0)
    col = lax.broadcasted_iota(I32, (CHUNK, CHUNK), 1)
    tril = (col <= row).astype(BF16)
    triu = (col >= row).astype(BF16)
    lane = lax.broadcasted_iota(I32, (CHUNK, LANES), 1)
    out_ref[...] = jnp.zeros(out_ref.shape, F32)
    for c in range(out_ref.shape[0]):
        g = g_ref[pl.ds(c * CHUNK, CHUNK), :]
        ls = jnp.minimum(g, 0.0) - jnp.log(1.0 + jnp.exp(-jnp.abs(g)))
        l1, l2, l3 = _split3(ls)
        pre = _dot(tril, l1) + _dot(tril, l2) + _dot(tril, l3)
        suf = _dot(triu, l1) + _dot(triu, l2) + _dot(triu, l3)
        bct = jnp.where((lane >= 4) & (lane < 8), pre,
                        jnp.where((lane >= 12) & (lane < 16), suf, g)).T
        for head in range(N_HEADS):
            for d in range(2):
                ich = 2 * N_HEADS * d + head
                fch = ich + N_HEADS
                out_ref[c, 2 * head + d, 0:1, :] = bct[ich:ich + 1, :]
                out_ref[c, 2 * head + d, 1:2, :] = bct[fch:fch + 1, :]


def _gate_sums(gates, chunks_per_step=8):
    n = gates.shape[0]
    nc = n // CHUNK
    cps = chunks_per_step
    return pl.pallas_call(
        _gates_kernel,
        grid=(nc // cps,),
        in_specs=[pl.BlockSpec((cps * CHUNK, LANES), lambda i: (i, 0))],
        out_specs=pl.BlockSpec((cps, 2 * N_HEADS, GATE_ROWS, CHUNK), lambda i: (i, 0, 0, 0)),
        out_shape=jax.ShapeDtypeStruct((nc, 2 * N_HEADS, GATE_ROWS, CHUNK), F32),
        compiler_params=_cparams(("arbitrary",)),
        name="gate_sums",
    )(gates)


def _mlstm_kernel(qkvo_ref, g_ref, c0_ref, n0_ref, m0_ref, hg_ref,
                  hm_ref, cf_ref, nf_ref, mf_ref, hf_s, hb_s, c_s, m_s, *, n_chunks):
    c_s[:, :, :HEAD_DIM, :] = c0_ref[0]
    c_s[:, :, HEAD_DIM:, :] = jnp.broadcast_to(n0_ref[0], (2, N_HEADS, STATE_PAD, HEAD_DIM))
    m_s[...] = m0_ref[0]
    src = lax.broadcasted_iota(I32, (CHUNK, CHUNK), 0)
    dst = lax.broadcasted_iota(I32, (CHUNK, CHUNK), 1)
    nt = (((1,), (1,)), ((), ()))

    zero_blk = jnp.zeros((CHUNK, HEAD_DIM), BF16)
    zero_pad = jnp.zeros((STATE_ROWS, HEAD_DIM), BF16)

    def diag2(x_a, x_b, zero):
        return jnp.concatenate([jnp.concatenate([x_a, zero], axis=1),
                                jnp.concatenate([zero, x_b], axis=1)], axis=0)

    def pair_direction(c, pair, d, prev):
        sl = pl.ds(pl.multiple_of(c * CHUNK, CHUNK), CHUNK)
        cols = lambda part: pl.ds((part * N_HEADS + 2 * pair) * HEAD_DIM, 2 * HEAD_DIM)
        q2 = qkvo_ref[0, sl, cols(0)]
        k2 = qkvo_ref[0, sl, cols(1)]
        vt2 = qkvo_ref[0, sl, cols(2)].astype(F32).T
        k_ab = [k2[:, :HEAD_DIM], k2[:, HEAD_DIM:]]
        vt_ab = [vt2[:HEAD_DIM, :], vt2[HEAD_DIM:, :]]
        mask = (src <= dst) if d == 0 else (src >= dst)
        rows = []
        for j in range(2):
            gates = g_ref[0, c, 2 * (2 * pair + j) + d]
            ig_row, b_row = gates[0:1, :], gates[1:2, :]
            gates_t = gates.T
            g = b_row[:, CHUNK - 1:CHUNK] if d == 0 else b_row[:, 0:1]
            w_log = g - b_row + ig_row
            m_loc = jnp.max(w_log, axis=1, keepdims=True)
            w = jnp.exp(w_log - m_loc)
            rows.append((ig_row, b_row, gates_t[:, 0:1], gates_t[:, 1:2], g, m_loc, w))
        c_bf = [prev[j][0].astype(BF16) for j in range(2)]
        lhs_q = jnp.concatenate([diag2(k_ab[0], k_ab[1], zero_blk), diag2(c_bf[0], c_bf[1], zero_pad)], axis=0)
        out_q = lax.dot_general(lhs_q, q2, nt, preferred_element_type=F32)
        s_list, extra = [], []
        for j in range(2):
            ig_row, b_row, ig_col, b_col, g, m_loc, w = rows[j]
            m_prev = prev[j][1]
            dmat = jnp.where(mask, b_row - b_col + ig_col, -jnp.inf)
            inter_log = b_row + m_prev
            m_comb = jnp.maximum(inter_log, jnp.max(dmat, axis=0, keepdims=True))
            s_t = out_q[j * CHUNK:(j + 1) * CHUNK, :] * jnp.exp(dmat - m_comb)
            s_list.append(s_t)
            extra.append((m_comb, jnp.exp(inter_log - m_comb)))
        s2 = jnp.concatenate(s_list, axis=0).astype(BF16)
        num2 = _dot(diag2(vt_ab[0].astype(BF16), vt_ab[1].astype(BF16), zero_blk), s2)
        lhs_k = [jnp.concatenate([vt_ab[j] * rows[j][6], jnp.broadcast_to(rows[j][6], (STATE_PAD, CHUNK))],
                                 axis=0).astype(BF16) for j in range(2)]
        k_rows = jnp.concatenate(k_ab, axis=0)
        c_loc2 = _dot(diag2(lhs_k[0], lhs_k[1], jnp.zeros((STATE_ROWS, CHUNK), BF16)), k_rows)
        res = []
        for j in range(2):
            ig_row, b_row, ig_col, b_col, g, m_loc, w = rows[j]
            c_prev, m_prev = prev[j]
            m_comb, w_inter = extra[j]
            cq = out_q[2 * CHUNK + j * STATE_ROWS:2 * CHUNK + (j + 1) * STATE_ROWS, :]
            num = num2[j * HEAD_DIM:(j + 1) * HEAD_DIM, :] + w_inter * cq[:HEAD_DIM, :]
            den = jnp.sum(s_list[j], axis=0, keepdims=True) + w_inter * cq[HEAD_DIM:HEAD_DIM + 1, :]
            h = num / jnp.maximum(jnp.abs(den), jnp.exp(-m_comb))
            m_new = jnp.maximum(g + m_prev, m_loc)
            a = jnp.exp(g + m_prev - m_new)
            bb = jnp.exp(m_loc - m_new)
            c_loc = c_loc2[j * STATE_ROWS:(j + 1) * STATE_ROWS, :]
            res.append((h, a * c_prev + bb * c_loc, m_new))
        return res

    def scan_body(i, carry):
        units = [(pair, d) for pair in range(N_HEADS // 2) for d in range(2)]
        prev = [[(c_s[d, 2 * pair + j], m_s[d, 2 * pair + j][:, 0:1]) for j in range(2)] for pair, d in units]
        chunk = [i, n_chunks - 1 - i]
        new = [pair_direction(chunk[d], pair, d, prev[u]) for u, (pair, d) in enumerate(units)]
        for (pair, d), res in zip(units, new):
            sl = pl.ds(pl.multiple_of(chunk[d] * CHUNK, CHUNK), CHUNK)
            for j, (h, c_new, m_new) in enumerate(res):
                head = 2 * pair + j
                (hf_s, hb_s)[d][head, :, sl] = h
                c_s[d, head] = c_new
                m_s[d, head] = jnp.broadcast_to(m_new, (1, LANES))
        return carry

    lax.fori_loop(0, n_chunks, scan_body, 0)

    def out_body(c, carry):
        sl = pl.ds(pl.multiple_of(c * CHUNK, CHUNK), CHUNK)
        for head in range(N_HEADS):
            hs = hf_s[head, :, sl] + hb_s[head, :, sl]
            hn = (hs * lax.rsqrt(jnp.mean(hs * hs, axis=0, keepdims=True) + EPS)).T * hg_ref[head]
            o = qkvo_ref[0, sl, pl.ds((3 * N_HEADS + head) * HEAD_DIM, HEAD_DIM)].astype(F32)
            hm_ref[0, sl, pl.ds(head * HEAD_DIM, HEAD_DIM)] = (hn * jax.nn.sigmoid(o)).astype(BF16)
        return carry

    lax.fori_loop(0, n_chunks, out_body, 0)
    cf_ref[0] = c_s[:, :, :HEAD_DIM, :]
    nf_ref[0] = c_s[:, :, HEAD_DIM:HEAD_DIM + 1, :]
    mf_ref[0] = m_s[...]


def _mlstm(qkvo3, gates5, c0, n0, m0, head_g3):
    b, t, w4 = qkvo3.shape
    nc = t // CHUNK
    once = dict(pipeline_mode=pl.Buffered(1)) if t * w4 * 2 > VMEM_LIMIT // 8 else {}
    st_c = pl.BlockSpec((1, 2, N_HEADS, HEAD_DIM, HEAD_DIM), lambda i: (i, 0, 0, 0, 0))
    st_v = pl.BlockSpec((1, 2, N_HEADS, 1, LANES), lambda i: (i, 0, 0, 0, 0))
    return pl.pallas_call(
        functools.partial(_mlstm_kernel, n_chunks=nc),
        grid=(b,),
        in_specs=[pl.BlockSpec((1, t, w4), lambda i: (i, 0, 0), **once),
                  pl.BlockSpec((1, nc, 2 * N_HEADS, GATE_ROWS, CHUNK), lambda i: (i, 0, 0, 0, 0), **once),
                  st_c, st_v, st_v,
                  pl.BlockSpec((N_HEADS, 1, HEAD_DIM), lambda i: (0, 0, 0))],
        out_specs=[pl.BlockSpec((1, t, MLSTM_WIDTH), lambda i: (i, 0, 0), **once), st_c, st_v, st_v],
        out_shape=[jax.ShapeDtypeStruct((b, t, MLSTM_WIDTH), BF16),
                   jax.ShapeDtypeStruct((b, 2, N_HEADS, HEAD_DIM, HEAD_DIM), F32),
                   jax.ShapeDtypeStruct((b, 2, N_HEADS, 1, LANES), F32),
                   jax.ShapeDtypeStruct((b, 2, N_HEADS, 1, LANES), F32)],
        scratch_shapes=[pltpu.VMEM((N_HEADS, HEAD_DIM, t), F32), pltpu.VMEM((N_HEADS, HEAD_DIM, t), F32),
                        pltpu.VMEM((2, N_HEADS, HEAD_DIM + STATE_PAD, HEAD_DIM), F32),
                        pltpu.VMEM((2, N_HEADS, 1, LANES), F32)],
        compiler_params=_cparams(("arbitrary",), VMEM_LIMIT),
        name="mlstm",
    )(qkvo3, gates5, c0, n0, m0, head_g3)


def _dft_cos_sin(n):
    k = np.arange(n)
    ang = 2.0 * np.pi * ((k[:, None] * k[None, :]) % n) / n
    return np.cos(ang), np.sin(ang)


def _channel_dft(scale):
    cd, sd = _dft_cos_sin(GROUP_DIM)
    eye = np.eye(N_GROUPS)
    return (jnp.asarray(np.kron(eye, cd) * scale, F32), jnp.asarray(np.kron(eye, -sd) * scale, F32))


def _fourier_ctx_kernel(u_ref, bdc_ref, bds_ref, ct_ref, st_ref, o_ref):
    u = u_ref[0]
    a = _dot3(u, bdc_ref[...])
    b = _dot3(u, bds_ref[...])
    o_ref[0] = (_dot3(ct_ref[...], a) + _dot3(st_ref[...], b)).astype(BF16)


def _fourier_ctx(u3):
    b, t, w = u3.shape
    bdc, bds = _channel_dft(1.0 / math.sqrt(t * GROUP_DIM))
    ct, st = _dft_cos_sin(t)
    const = lambda i: (0, 0)
    return pl.pallas_call(
        _fourier_ctx_kernel,
        grid=(b,),
        in_specs=[pl.BlockSpec((1, t, w), lambda i: (i, 0, 0)),
                  pl.BlockSpec((w, w), const), pl.BlockSpec((w, w), const),
                  pl.BlockSpec((t, t), const), pl.BlockSpec((t, t), const)],
        out_specs=pl.BlockSpec((1, t, w), lambda i: (i, 0, 0)),
        out_shape=jax.ShapeDtypeStruct((b, t, w), BF16),
        compiler_params=_cparams(("arbitrary",)),
        name="fourier_ctx",
    )(u3, bdc, bds, jnp.asarray(ct, F32), jnp.asarray(st, F32))


def _fourier_chan_kernel(u_ref, bdc_ref, bds_ref, a_ref, b_ref):
    u = u_ref[0].astype(BF16)
    a_ref[0] = _dot(u, bdc_ref[...]).astype(BF16)
    b_ref[0] = _dot(u, bds_ref[...]).astype(BF16)


def _fourier_pos_kernel(cre_ref, sre_ref, cct_ref, sct_ref, a_ref, b_ref, o_ref, cp_s, sp_s):
    @pl.when(pl.program_id(1) == 0)
    def _():
        gw = cct_ref.shape[0]
        for q in range(cre_ref.shape[0]):
            cr, sr = cre_ref[q], sre_ref[q]
            cc, sc = cct_ref[...], sct_ref[...]
            cp_s[q * gw:(q + 1) * gw, :] = (cr * cc - sr * sc).astype(BF16)
            sp_s[q * gw:(q + 1) * gw, :] = (sr * cc + cr * sc).astype(BF16)

    o_ref[0] = (_dot(cp_s[...], a_ref[0]) + _dot(sp_s[...], b_ref[0])).astype(BF16)


def _fourier_lat(u3, grid_w):
    b, t, w = u3.shape
    rows = t // grid_w
    bdc, bds = _channel_dft(1.0 / math.sqrt(t * GROUP_DIM))
    tm = 512
    const = lambda i, j: (0, 0)
    a, bm = pl.pallas_call(
        _fourier_chan_kernel,
        grid=(b, t // tm),
        in_specs=[pl.BlockSpec((1, tm, w), lambda i, j: (i, j, 0)),
                  pl.BlockSpec((w, w), const), pl.BlockSpec((w, w), const)],
        out_specs=[pl.BlockSpec((1, tm, w), lambda i, j: (i, j, 0))] * 2,
        out_shape=[jax.ShapeDtypeStruct((b, t, w), BF16)] * 2,
        compiler_params=_cparams(("arbitrary", "arbitrary")),
        name="fourier_chan",
    )(u3, bdc.astype(BF16), bds.astype(BF16))
    cr, sr = _dft_cos_sin(rows)
    cc, sc = _dft_cos_sin(grid_w)
    cre = jnp.asarray(np.repeat(cr, grid_w, axis=1)[:, None, :], F32)
    sre = jnp.asarray(np.repeat(sr, grid_w, axis=1)[:, None, :], F32)
    cct = jnp.asarray(np.tile(cc, (1, rows)), F32)
    sct = jnp.asarray(np.tile(sc, (1, rows)), F32)
    rpt = tm // grid_w
    return pl.pallas_call(
        _fourier_pos_kernel,
        grid=(t // tm, b),
        in_specs=[pl.BlockSpec((rpt, 1, t), lambda j, i: (j, 0, 0)),
                  pl.BlockSpec((rpt, 1, t), lambda j, i: (j, 0, 0)),
                  pl.BlockSpec((grid_w, t), lambda j, i: (0, 0)),
                  pl.BlockSpec((grid_w, t), lambda j, i: (0, 0)),
                  pl.BlockSpec((1, t, w), lambda j, i: (i, 0, 0)),
                  pl.BlockSpec((1, t, w), lambda j, i: (i, 0, 0))],
        out_specs=pl.BlockSpec((1, tm, w), lambda j, i: (i, j, 0)),
        out_shape=jax.ShapeDtypeStruct((b, t, w), BF16),
        scratch_shapes=[pltpu.VMEM((tm, t), BF16), pltpu.VMEM((tm, t), BF16)],
        compiler_params=_cparams(("arbitrary", "arbitrary"), VMEM_LIMIT),
        name="fourier_pos",
    )(cre, sre, cct, sct, a, bm)


def _out_kernel(hmc_ref, fuc_ref, xc_ref, hml_ref, ful_ref, xl_ref, mod_ref, g_ref, wa_ref, wb_ref, rw_ref,
                x1_ref, aug_ref, afft_ref, *, ctx_tiles):
    def tile(hm_ref, fu_ref, x_ref):
        a = _dot(hm_ref[...], wa_ref[...]) + _dot(fu_ref[...], wb_ref[...])
        x1 = x_ref[...] + mod_ref[0, 2:3, :] * a
        x1_ref[...] = x1
        ms = jnp.mean(x1 * x1, axis=-1, keepdims=True)
        y = x1 * lax.rsqrt(ms + EPS) * g_ref[...]
        h2 = y * (1.0 + mod_ref[0, 4:5, :]) + mod_ref[0, 3:4, :]
        h_hi, h_lo, _ = _split3(h2)
        w_hi, w_lo, _ = _split3(rw_ref[...])
        both = _dot(h_hi, jnp.concatenate([w_hi, w_lo], axis=1))
        logits = both[:, :LANES] + (both[:, LANES:] + _dot(h_lo, w_hi))
        lane = lax.broadcasted_iota(I32, logits.shape, 1)
        valid = lane < N_EXPERTS
        lg = jnp.where(valid, logits, -1e30)
        ex = jnp.where(valid, jnp.exp(lg - jnp.max(lg, axis=1, keepdims=True)), 0.0)
        aff = ex / jnp.sum(ex, axis=1, keepdims=True)
        aug_ref[:, :D_MODEL] = h2
        aug_ref[:, D_MODEL:] = aff
        afft_ref[...] = aff.T[:N_EXPERTS, :]

    @pl.when(pl.program_id(0) < ctx_tiles)
    def _():
        tile(hmc_ref, fuc_ref, xc_ref)

    @pl.when(pl.program_id(0) >= ctx_tiles)
    def _():
        tile(hml_ref, ful_ref, xl_ref)


def _out_proj(ctx, lat, mod, lat_tiles_per_batch, tm, g2, wa, wb, rw):
    n_c, n_l = ctx[2].shape[0], lat[2].shape[0]
    n = n_c + n_l
    ct = n_c // tm
    const = lambda i: (0, 0)
    ctx_tile = lambda i: (jnp.minimum(i, ct - 1), 0)
    lat_tile = lambda i: (jnp.maximum(i - ct, 0), 0)
    widths = (MLSTM_WIDTH, FOURIER_WIDTH, D_MODEL)
    return pl.pallas_call(
        functools.partial(_out_kernel, ctx_tiles=ct),
        grid=(n // tm,),
        in_specs=[pl.BlockSpec((tm, w), ctx_tile) for w in widths]
                 + [pl.BlockSpec((tm, w), lat_tile) for w in widths]
                 + [pl.BlockSpec((1, N_MOD, D_MODEL),
                                 lambda i: (jnp.where(i < ct, 0, 1 + (i - ct) // lat_tiles_per_batch), 0, 0)),
                    pl.BlockSpec((1, D_MODEL), const),
                    pl.BlockSpec((MLSTM_WIDTH, D_MODEL), const),
                    pl.BlockSpec((FOURIER_WIDTH, D_MODEL), const),
                    pl.BlockSpec((D_MODEL, LANES), const)],
        out_specs=[pl.BlockSpec((tm, D_MODEL), lambda i: (i, 0)),
                   pl.BlockSpec((tm, AUG), lambda i: (i, 0)),
                   pl.BlockSpec((N_EXPERTS, tm), lambda i: (0, i))],
        out_shape=[jax.ShapeDtypeStruct((n, D_MODEL), F32),
                   jax.ShapeDtypeStruct((n, AUG), F32),
                   jax.ShapeDtypeStruct((N_EXPERTS, n), F32)],
        compiler_params=_cparams(("arbitrary",), VMEM_LIMIT),
        name="out_proj",
    )(*ctx, *lat, mod, g2, wa, wb, rw)


def _topk_kernel(aff_ref, pos_ref, blk_ref, rng_ref, *, n, cap):
    nb = n // LANES
    aff = aff_ref[...]

    def enough(t):
        return jnp.sum(jnp.where(aff >= t, 1.0, 0.0), axis=1, keepdims=True) >= cap

    def pow2(k):
        return lax.bitcast_convert_type(jnp.left_shift(127 - k, 23), F32)

    def exp_search(_, c):
        lo, hi = c
        mid = jnp.right_shift(lo + hi, 1)
        ok = enough(pow2(jnp.minimum(mid, 126)))
        return jnp.where(ok, lo, mid + 1), jnp.where(ok, mid, hi)

    kz = jnp.zeros((N_EXPERTS, 1), I32)
    kstar, _ = lax.fori_loop(0, 7, exp_search, (kz, kz + 127))
    found = kstar < 127
    p = pow2(jnp.minimum(kstar, 126))
    t_lo0 = jnp.where(found, p, 0.0)
    t_hi0 = jnp.where(found, 2.0 * p, p)

    def bisect(_, c):
        t_lo, t_hi = c
        mid = 0.5 * (t_lo + t_hi)
        ok = enough(mid)
        return jnp.where(ok, mid, t_lo), jnp.where(ok, t_hi, mid)

    t_lo, t_hi = lax.fori_loop(0, 32, bisect, (t_lo0, t_hi0))
    n_gt = jnp.sum(jnp.where(aff >= t_hi, 1.0, 0.0), axis=1, keepdims=True)
    need = cap - n_gt
    row = lax.broadcasted_iota(I32, (LANES, LANES), 0)
    col = lax.broadcasted_iota(I32, (LANES, LANES), 1)
    upper = (row <= col).astype(BF16)
    blane = lax.broadcasted_iota(I32, blk_ref.shape, 1)

    def block(b, carry):
        eq_off, sel_off = carry
        sl = pl.ds(pl.multiple_of(b * LANES, LANES), LANES)
        ab = aff_ref[:, sl]
        gt = ab >= t_hi
        eq = (ab >= t_lo) & (ab < t_hi)
        eqf = jnp.where(eq, 1.0, 0.0)
        eq_rank = _dot(eqf.astype(BF16), upper) + eq_off - eqf
        sel = gt | (eq & (eq_rank < need))
        self_ = jnp.where(sel, 1.0, 0.0)
        cum = _dot(self_.astype(BF16), upper) + sel_off
        pos_ref[:, sl] = jnp.where(sel, cum - 1.0, -1.0).astype(I32)
        blk_ref[...] = jnp.where(blane == b, sel_off.astype(I32), blk_ref[...])
        return (eq_off + jnp.sum(eqf, axis=1, keepdims=True),
                sel_off + jnp.sum(self_, axis=1, keepdims=True))

    blk_ref[...] = jnp.full(blk_ref.shape, cap, I32)
    zero = jnp.zeros((N_EXPERTS, 1), F32)
    lax.fori_loop(0, nb, block, (zero, zero))

    blk = blk_ref[...]
    rlane = lax.broadcasted_iota(I32, rng_ref.shape, 1)
    rng = jnp.zeros(rng_ref.shape, I32)
    for jc in range(cap // LANES):
        j0 = jc * LANES
        ends_before = jnp.where((blane >= 1) & (blane <= nb) & (blk <= j0), 1.0, 0.0)
        starts_below = jnp.where((blane < nb) & (blk < j0 + LANES), 1.0, 0.0)
        rng = jnp.where(rlane == jc, jnp.sum(ends_before, axis=1, keepdims=True).astype(I32), rng)
        rng = jnp.where(rlane == RANGE_HALF + jc, jnp.sum(starts_below, axis=1, keepdims=True).astype(I32), rng)
    rng_ref[...] = rng


def _topk(aff_t, cap):
    n = aff_t.shape[1]
    assert cap // LANES <= RANGE_HALF
    return pl.pallas_call(
        functools.partial(_topk_kernel, n=n, cap=cap),
        grid=(1,),
        in_specs=[pl.BlockSpec((N_EXPERTS, n), lambda i: (0, 0))],
        out_specs=[pl.BlockSpec((N_EXPERTS, n), lambda i: (0, 0)),
                   pl.BlockSpec((N_EXPERTS, 2 * LANES), lambda i: (0, 0)),
                   pl.BlockSpec((N_EXPERTS, LANES), lambda i: (0, 0))],
        out_shape=[jax.ShapeDtypeStruct((N_EXPERTS, n), I32),
                   jax.ShapeDtypeStruct((N_EXPERTS, 2 * LANES), I32),
                   jax.ShapeDtypeStruct((N_EXPERTS, LANES), I32)],
        compiler_params=_cparams(("arbitrary",)),
        name="topk",
    )(aff_t)


def _compact_kernel(rng_sm, pos_ref, out_ref):
    e = pl.program_id(0)
    slot0 = lax.broadcasted_iota(I32, (LANES, LANES), 0)
    tok = lax.broadcasted_iota(I32, (LANES, LANES), 1)

    def chunk(jc, carry):
        slot = slot0 + jc * LANES

        def body(b, acc):
            prow = pos_ref[0, :, pl.ds(pl.multiple_of(b * LANES, LANES), LANES)]
            return acc + jnp.where(prow == slot, (tok + b * LANES).astype(F32), 0.0)

        acc = lax.fori_loop(rng_sm[e, jc], rng_sm[e, RANGE_HALF + jc], body, jnp.zeros((LANES, LANES), F32))
        out_ref[0, jc] = jnp.sum(acc.T, axis=0, keepdims=True).astype(I32)
        return carry

    lax.fori_loop(0, out_ref.shape[1], chunk, 0)


def _compact(pos, rng, cap):
    n = pos.shape[1]
    ncj = cap // LANES
    out = pl.pallas_call(
        _compact_kernel,
        grid_spec=pltpu.PrefetchScalarGridSpec(
            num_scalar_prefetch=1,
            grid=(N_EXPERTS,),
            in_specs=[pl.BlockSpec((1, 1, n), lambda e, rng: (e, 0, 0))],
            out_specs=pl.BlockSpec((1, ncj, 1, LANES), lambda e, rng: (e, 0, 0, 0)),
        ),
        out_shape=jax.ShapeDtypeStruct((N_EXPERTS, ncj, 1, LANES), I32),
        compiler_params=_cparams(("arbitrary",)),
        name="compact",
    )(rng, pos.reshape(N_EXPERTS, 1, n))
    return out.reshape(N_EXPERTS, cap)


def _moe_kernel(lists_sm, aug, w1_ref, w3_ref, w2_ref, ye_ref,
                xbuf, xe, gate, act_all, w2_all, w1b, w3b, sem, *, rows, row_chunk, unroll, tf, nf_static):
    e = pl.program_id(0)
    f = pl.program_id(1)
    ne = pl.num_programs(0)
    nf = pl.num_programs(1)

    def start_row(list_pos, j):
        pltpu.make_async_copy(aug.at[pl.ds(lists_sm[list_pos], 1)], xbuf.at[pl.ds(j, 1)], sem.at[0]).start()

    def wait_rows():
        pltpu.make_async_copy(xbuf, xbuf, sem.at[0]).wait()

    @pl.when((e == 0) & (f == 0))
    def _():
        def body(g, carry):
            for u in range(unroll):
                start_row(g * unroll + u, g * unroll + u)
            return carry

        lax.fori_loop(0, rows // unroll, body, 0)

    @pl.when(f == 0)
    def _():
        wait_rows()
        xe[...] = xbuf[:, :D_MODEL].astype(BF16)
        lane = lax.broadcasted_iota(I32, (rows, LANES), 1)
        gate[...] = jnp.sum(jnp.where(lane == e, xbuf[:, D_MODEL:], 0.0), axis=1, keepdims=True)
        ye_ref[0, rows:, :] = jnp.zeros((ye_ref.shape[1] - rows, D_MODEL), BF16)

    w1b[...] = w1_ref[0].astype(BF16)
    w3b[...] = w3_ref[0].astype(BF16)
    fsl = pl.ds(pl.multiple_of(f * tf, tf), tf)
    w2_all[fsl, :] = w2_ref[0].astype(BF16)
    n_chunks = rows // row_chunk
    up_share = 3 * rows // (4 * nf_static * n_chunks)
    down_share = rows // (4 * n_chunks)
    nxt = (e + 1) * rows
    for r in range(n_chunks):
        sl = pl.ds(r * row_chunk, row_chunk)
        x = xe[sl, :]
        a = _dot(x, w1b[...])
        bg = _dot(x, w3b[...])
        act_all[sl, fsl] = (a * jax.nn.sigmoid(a) * bg).astype(BF16)
        for u in range(up_share):
            j = (f * n_chunks + r) * up_share + u
            start_row(nxt + j, j)

    @pl.when(f == nf - 1)
    def _():
        for r in range(n_chunks):
            sl = pl.ds(r * row_chunk, row_chunk)
            ye_ref[0, sl, :] = (_dot(act_all[sl, :], w2_all[...]) * gate[sl, :]).astype(BF16)
            for u in range(down_share):
                j = 3 * rows // 4 + r * down_share + u
                start_row(nxt + j, j)

        @pl.when(e == ne - 1)
        def _():
            wait_rows()


def _moe(lists, aug, w1, w3, w2, rows, tf=256, row_chunk=512, unroll=8):
    nf = D_EXPERT // tf
    assert rows % (4 * nf * (rows // row_chunk)) == 0 and rows % unroll == 0
    return pl.pallas_call(
        functools.partial(_moe_kernel, rows=rows, row_chunk=row_chunk, unroll=unroll, tf=tf, nf_static=nf),
        grid_spec=pltpu.PrefetchScalarGridSpec(
            num_scalar_prefetch=1,
            grid=(N_EXPERTS, nf),
            in_specs=[pl.BlockSpec(memory_space=pl.ANY),
                      pl.BlockSpec((1, D_MODEL, tf), lambda e, f, ls: (e, 0, f)),
                      pl.BlockSpec((1, D_MODEL, tf), lambda e, f, ls: (e, 0, f)),
                      pl.BlockSpec((1, tf, D_MODEL), lambda e, f, ls: (e, f, 0))],
            out_specs=pl.BlockSpec((1, rows + WINDOW, D_MODEL), lambda e, f, ls: (e, 0, 0)),
            scratch_shapes=[pltpu.VMEM((rows, AUG), F32), pltpu.VMEM((rows, D_MODEL), BF16),
                            pltpu.VMEM((rows, 1), F32),
                            pltpu.VMEM((rows, D_EXPERT), BF16), pltpu.VMEM((D_EXPERT, D_MODEL), BF16),
                            pltpu.VMEM((D_MODEL, tf), BF16), pltpu.VMEM((D_MODEL, tf), BF16),
                            pltpu.SemaphoreType.DMA((1,))],
        ),
        out_shape=jax.ShapeDtypeStruct((N_EXPERTS, rows + WINDOW, D_MODEL), BF16),
        compiler_params=_cparams(("arbitrary", "arbitrary"), VMEM_LIMIT),
        name="moe",
    )(lists, aug, w1, w3, w2)


def _combine_kernel(seg_sm, x1_ref, pos_ref, mod_ref, g_ref, ye_hbm, o_ref, wbuf, y_s, sem, *, base, tm):
    i = pl.program_id(0)
    nt = pl.num_programs(0)
    bpt = tm // LANES
    slot = lax.rem(i, 2)
    last_start = ye_hbm.shape[1] - WINDOW
    tn = (((0,), (0,)), ((), ()))

    def start_windows(tile, rnd, buf):
        for e in range(N_EXPERTS):
            start = jnp.minimum(base + (seg_sm[e, tile * bpt] & -ROW_ALIGN) + rnd * WINDOW, last_start)
            pltpu.make_async_copy(ye_hbm.at[e, pl.ds(pl.multiple_of(start, ROW_ALIGN), WINDOW)],
                                  wbuf.at[buf, pl.ds(e * WINDOW, WINDOW)], sem.at[buf]).start()

    def wait_windows(buf):
        pltpu.make_async_copy(wbuf.at[buf], wbuf.at[buf], sem.at[buf]).wait()

    @pl.when(i == 0)
    def _():
        start_windows(i, 0, slot)

    most = jnp.int32(0)
    for e in range(N_EXPERTS):
        most = jnp.maximum(most, seg_sm[e, (i + 1) * bpt] - (seg_sm[e, i * bpt] & -ROW_ALIGN))
    n_rounds = jnp.right_shift(most + (WINDOW - 1), WINDOW_SHIFT)

    pos = pos_ref[...]
    valid = pos >= 0
    first = jnp.min(jnp.where(valid, pos, jnp.int32(2 ** 30)), axis=1, keepdims=True)
    rel = pos - (first & -ROW_ALIGN)
    lane = lax.broadcasted_iota(I32, (N_EXPERTS, N_EXPERTS * WINDOW), 1)
    owner = lax.broadcasted_iota(I32, (N_EXPERTS, N_EXPERTS * WINDOW), 0)
    spread = jnp.where(jnp.right_shift(lane, WINDOW_SHIFT) == owner, 1.0, 0.0).astype(BF16)
    wrow = (lax.broadcasted_iota(I32, (1, N_EXPERTS * WINDOW), 1) & (WINDOW - 1)).astype(F32)

    def place(rnd, first_round):
        q = jnp.where(valid & (rel >= rnd * WINDOW) & (rel < (rnd + 1) * WINDOW), rel - rnd * WINDOW, -1)
        q = q.astype(F32).astype(BF16)
        for part in range(tm // LANES):
            tok = pl.ds(part * LANES, LANES)
            qb = lax.dot_general(q[:, part * LANES:(part + 1) * LANES], spread, tn, preferred_element_type=F32)
            onehot = jnp.where(qb == wrow, 1.0, 0.0).astype(BF16)
            rows_y = _dot(onehot, wbuf[slot])
            if first_round:
                y_s[tok, :] = rows_y
            else:
                y_s[tok, :] += rows_y

    wait_windows(slot)
    start_windows(jnp.minimum(i + 1, nt - 1), 0, 1 - slot)
    place(0, True)

    def extra_round(rnd, carry):
        start_windows(i, rnd, slot)
        wait_windows(slot)
        place(rnd, False)
        return carry

    lax.fori_loop(1, n_rounds, extra_round, 0)
    x = x1_ref[...] + mod_ref[0, 5:6, :] * y_s[...]
    ms = jnp.mean(x * x, axis=-1, keepdims=True)
    o_ref[...] = x * lax.rsqrt(ms + EPS) * g_ref[...]

    @pl.when(i == nt - 1)
    def _():
        wait_windows(1 - slot)


def _combine_norm(seg, x1, first_row, pos, ye, base, mod, row0, tiles_per_batch, tm, gf):
    n = pos.shape[1]
    tile0 = first_row // tm
    return pl.pallas_call(
        functools.partial(_combine_kernel, base=base, tm=tm),
        grid_spec=pltpu.PrefetchScalarGridSpec(
            num_scalar_prefetch=1,
            grid=(n // tm,),
            in_specs=[pl.BlockSpec((tm, D_MODEL), lambda i, sg: (i + tile0, 0)),
                      pl.BlockSpec((N_EXPERTS, tm), lambda i, sg: (0, i)),
                      pl.BlockSpec((1, N_MOD, D_MODEL),
                                   lambda i, sg: (row0 + i // tiles_per_batch, 0, 0)),
                      pl.BlockSpec((1, D_MODEL), lambda i, sg: (0, 0)),
                      pl.BlockSpec(memory_space=pl.ANY)],
            out_specs=pl.BlockSpec((tm, D_MODEL), lambda i, sg: (i, 0)),
            scratch_shapes=[pltpu.VMEM((2, N_EXPERTS * WINDOW, D_MODEL), BF16),
                            pltpu.VMEM((tm, D_MODEL), F32),
                            pltpu.SemaphoreType.DMA((2,))],
        ),
        out_shape=jax.ShapeDtypeStruct((n, D_MODEL), F32),
        compiler_params=_cparams(("arbitrary",), VMEM_LIMIT),
        name="combine_norm",
    )(seg, x1, pos, mod, gf, ye)


def kernel(x_prompt, x_sample, state_C, state_n, state_m, c, c_ctx, ada_w, ada_b, norm1_g, norm2_g,
           w_in, b_in, head_g, w_out, router_w, exp_w1, exp_w3, exp_w2, final_g):
    depth = ada_w.shape[0]
    assert depth == 1, "single-layer trunk"
    bc_, tc_, _ = x_prompt.shape
    bl_, tl_, _ = x_sample.shape
    w4 = 4 * MLSTM_WIDTH
    tm = 512

    cvec = jnp.concatenate([c_ctx[None, :], c, jnp.zeros((8 - 1 - bl_, D_MODEL), F32)], axis=0)
    mod = _modulation(cvec, ada_w[0], ada_b[0])

    wi = w_in[0]
    bi = b_in[0]
    wq = wi[:, :w4].astype(BF16)
    bq = bi[None, :w4]
    wg = jnp.pad(wi[:, w4:w4 + 16], ((0, 0), (0, LANES - 16))).astype(BF16)
    bg = jnp.pad(bi[w4:w4 + 16], (0, LANES - 16))[None, :]
    wf = wi[:, w4 + 16:].astype(BF16)
    bfo = bi[None, w4 + 16:]
    colscale = jnp.ones((1, w4), F32).at[:, MLSTM_WIDTH:2 * MLSTM_WIDTH].set(HEAD_DIM ** -0.5)
    g1 = norm1_g[0][None, :]
    g2 = norm2_g[0][None, :]
    wo = w_out[0].astype(BF16)
    wa, wb = wo[:MLSTM_WIDTH], wo[MLSTM_WIDTH:]
    rw = jnp.pad(router_w[0], ((0, 0), (0, LANES - N_EXPERTS)))
    hg3 = head_g[0][:, None, :]

    def spread(v):
        return jnp.broadcast_to(v[..., None, None], v.shape + (1, LANES))

    n_ctx, n_lat = bc_ * tc_, bl_ * tl_

    def mixer(x3, row0, init, grid_w):
        b, t, _ = x3.shape
        x2 = x3.reshape(b * t, D_MODEL)
        tpb = max(t // tm, 1) if row0 else (b * t) // tm + 1
        qkvo, gates, u = _in_proj(x2, mod, row0, tpb, tm, g1, wq, bq, colscale, wg, bg, wf, bfo)
        gate_rows = _gate_sums(gates).reshape(b, t // CHUNK, 2 * N_HEADS, GATE_ROWS, CHUNK)
        c0, n0, m0 = init
        hm, cf, nf, mf = _mlstm(qkvo.reshape(b, t, w4), gate_rows, c0, n0, m0, hg3)
        u3 = u.reshape(b, t, FOURIER_WIDTH)
        fu = _fourier_ctx(u3) if grid_w is None else _fourier_lat(u3, grid_w)
        return (hm.reshape(b * t, MLSTM_WIDTH), fu.reshape(b * t, FOURIER_WIDTH), x2), (cf, nf, mf)

    zero_init = (jnp.zeros((bc_, 2, N_HEADS, HEAD_DIM, HEAD_DIM), F32),
                 jnp.zeros((bc_, 2, N_HEADS, 1, LANES), F32),
                 jnp.zeros((bc_, 2, N_HEADS, 1, LANES), F32))
    lat_init = (state_C[:, 0], state_n[:, 0][..., None, :], spread(state_m[:, 0]))
    ctx_parts, (cf, nf, mf) = mixer(x_prompt, 0, zero_init, None)
    lat_parts, _ = mixer(x_sample, 1, lat_init, GRID_W)
    tmo = 256
    x1, table, aff_t = _out_proj(ctx_parts, lat_parts, mod, tl_ // tmo, tmo, g2, wa, wb, rw)

    capc = CAPACITY_FACTOR * n_ctx // N_EXPERTS
    capl = CAPACITY_FACTOR * n_lat // N_EXPERTS
    posc, blkc, rngc = _topk(aff_t[:, :n_ctx], capc)
    posl, blkl, rngl = _topk(aff_t[:, n_ctx:], capl)
    lists = jnp.concatenate([_compact(posc, rngc, capc), _compact(posl, rngl, capl) + n_ctx], axis=1)
    lists = jnp.concatenate([lists, jnp.zeros((1, capc + capl), I32)], axis=0).reshape(-1)
    ye = _moe(lists, table, exp_w1[0], exp_w3[0], exp_w2[0], capc + capl)
    gf = final_g[None, :]
    tc = 256
    y_prompt = _combine_norm(blkc, x1, 0, posc, ye, 0, mod, 0, n_ctx // tc + 1, tc, gf)
    y_sample = _combine_norm(blkl, x1, n_ctx, posl, ye, capc, mod, 1, tl_ // tc, tc, gf)
    y_prompt = y_prompt.reshape(x_prompt.shape)
    y_sample = y_sample.reshape(x_sample.shape)
    new_c = cf[:, None]
    new_n = nf[:, None, :, :, 0, :]
    new_m = mf[:, None, :, :, 0, 0]
    return (y_prompt, y_sample, new_c, new_n, new_m)
```

```python
import functools
import math

import numpy as np
import jax
import jax.numpy as jnp
from jax import lax
from jax.experimental import pallas as pl
from jax.experimental.pallas import tpu as pltpu

F32 = jnp.float32
BF16 = jnp.bfloat16
I32 = jnp.int32

D_MODEL = 1024
N_HEADS = 4
HEAD_DIM = 128
MLSTM_WIDTH = N_HEADS * HEAD_DIM
FOURIER_WIDTH = 512
N_GROUPS = 4
GROUP_DIM = 128
CHUNK = 128
GRID_W = 64
N_EXPERTS = 16
CAPACITY_FACTOR = 2
D_EXPERT = 2048
N_MOD = 6
EPS = 1e-6
LANES = 128
GATE_ROWS = 8
STATE_PAD = 16
STATE_ROWS = HEAD_DIM + STATE_PAD
ROW_ALIGN = 16
AUG = D_MODEL + LANES
RANGE_HALF = 64
WINDOW_SHIFT = 6
WINDOW = 1 << WINDOW_SHIFT
VMEM_LIMIT = 56 * 1024 * 1024


def _cparams(sem, vmem=None):
    return pltpu.CompilerParams(dimension_semantics=sem, vmem_limit_bytes=vmem)


def _split3(a):
    a1 = a.astype(BF16)
    r1 = a - a1.astype(F32)
    a2 = r1.astype(BF16)
    a3 = (r1 - a2.astype(F32)).astype(BF16)
    return a1, a2, a3


def _dot(a, b):
    return jnp.dot(a, b, preferred_element_type=F32)


def _dot3(a, b):
    a1, a2, _ = _split3(a)
    b1, b2, _ = _split3(b)
    return _dot(a1, b1) + (_dot(a1, b2) + _dot(a2, b1))


def _mod_kernel(c_ref, w_ref, b_ref, o_ref):
    c = c_ref[...]
    s = c * jax.nn.sigmoid(c)
    o_ref[...] = _dot(s.astype(BF16), w_ref[...].astype(BF16)) + b_ref[...]


def _modulation(cvec8, ada_w, ada_b):
    tn = D_MODEL
    out = pl.pallas_call(
        _mod_kernel,
        grid=(N_MOD,),
        in_specs=[pl.BlockSpec((8, D_MODEL), lambda j: (0, 0)),
                  pl.BlockSpec((D_MODEL, tn), lambda j: (0, j)),
                  pl.BlockSpec((1, tn), lambda j: (0, j))],
        out_specs=pl.BlockSpec((8, tn), lambda j: (0, j)),
        out_shape=jax.ShapeDtypeStruct((8, N_MOD * D_MODEL), F32),
        compiler_params=_cparams(("arbitrary",)),
        name="modulation",
    )(cvec8, ada_w, ada_b.reshape(1, -1))
    return out.reshape(8, N_MOD, D_MODEL)


def _mod_index(row0, tiles_per_batch):
    return lambda i: (row0 + i // tiles_per_batch, 0, 0)


def _in_kernel(x_ref, mod_ref, g_ref, wq_ref, bq_ref, sc_ref, wg_ref, bg_ref, wf_ref, bf_ref,
               qkvo_ref, gates_ref, u_ref):
    x = x_ref[...]
    ms = jnp.mean(x * x, axis=-1, keepdims=True)
    y = x * lax.rsqrt(ms + EPS) * g_ref[...]
    h = y * (1.0 + mod_ref[0, 1:2, :]) + mod_ref[0, 0:1, :]
    hb = h.astype(BF16)
    p = (_dot(hb, wq_ref[...]) + bq_ref[...]) * sc_ref[...]
    qkvo_ref[...] = p.astype(BF16)
    gates_ref[...] = _dot(hb, wg_ref[...]) + bg_ref[...]
    u_ref[...] = _dot(hb, wf_ref[...]) + bf_ref[...]


def _in_proj(x2, mod, row0, tiles_per_batch, tm, g1, wq, bq, sc, wg, bg, wf, bfo):
    n = x2.shape[0]
    wq_n = wq.shape[1]
    const = lambda i: (0, 0)
    return pl.pallas_call(
        _in_kernel,
        grid=(n // tm,),
        in_specs=[pl.BlockSpec((tm, D_MODEL), lambda i: (i, 0)),
                  pl.BlockSpec((1, N_MOD, D_MODEL), _mod_index(row0, tiles_per_batch)),
                  pl.BlockSpec((1, D_MODEL), const),
                  pl.BlockSpec((D_MODEL, wq_n), const), pl.BlockSpec((1, wq_n), const),
                  pl.BlockSpec((1, wq_n), const),
                  pl.BlockSpec((D_MODEL, LANES), const), pl.BlockSpec((1, LANES), const),
                  pl.BlockSpec((D_MODEL, FOURIER_WIDTH), const), pl.BlockSpec((1, FOURIER_WIDTH), const)],
        out_specs=[pl.BlockSpec((tm, wq_n), lambda i: (i, 0)),
                   pl.BlockSpec((tm, LANES), lambda i: (i, 0)),
                   pl.BlockSpec((tm, FOURIER_WIDTH), lambda i: (i, 0))],
        out_shape=[jax.ShapeDtypeStruct((n, wq_n), BF16),
                   jax.ShapeDtypeStruct((n, LANES), F32),
                   jax.ShapeDtypeStruct((n, FOURIER_WIDTH), F32)],
        compiler_params=_cparams(("arbitrary",), VMEM_LIMIT),
        name="in_proj",
    )(x2, mod, g1, wq, bq, sc, wg, bg, wf, bfo)


def _gates_kernel(g_ref, out_ref):
    row = lax.broadcasted_iota(I32, (CHUNK, CHUNK), 0)
    col = lax.broadcasted_iota(I32, (CHUNK, CHUNK), 1)
    tril = (col <= row).astype(BF16)
    triu = (col >= row).astype(BF16)
    lane = lax.broadcasted_iota(I32, (CHUNK, LANES), 1)
    out_ref[...] = jnp.zeros(out_ref.shape, F32)
    for c in range(out_ref.shape[0]):
        g = g_ref[pl.ds(c * CHUNK, CHUNK), :]
        ls = jnp.minimum(g, 0.0) - jnp.log(1.0 + jnp.exp(-jnp.abs(g)))
        l1, l2, l3 = _split3(ls)
        pre = _dot(tril, l1) + _dot(tril, l2) + _dot(tril, l3)
        suf = _dot(triu, l1) + _dot(triu, l2) + _dot(triu, l3)
        bct = jnp.where((lane >= 4) & (lane < 8), pre,
                        jnp.where((lane >= 12) & (lane < 16), suf, g)).T
        for head in range(N_HEADS):
            for d in range(2):
                ich = 2 * N_HEADS * d + head
                fch = ich + N_HEADS
                out_ref[c, 2 * head + d, 0:1, :] = bct[ich:ich + 1, :]
                out_ref[c, 2 * head + d, 1:2, :] = bct[fch:fch + 1, :]


def _gate_sums(gates, chunks_per_step=8):
    n = gates.shape[0]
    nc = n // CHUNK
    cps = chunks_per_step
    return pl.pallas_call(
        _gates_kernel,
        grid=(nc // cps,),
        in_specs=[pl.BlockSpec((cps * CHUNK, LANES), lambda i: (i, 0))],
        out_specs=pl.BlockSpec((cps, 2 * N_HEADS, GATE_ROWS, CHUNK), lambda i: (i, 0, 0, 0)),
        out_shape=jax.ShapeDtypeStruct((nc, 2 * N_HEADS, GATE_ROWS, CHUNK), F32),
        compiler_params=_cparams(("arbitrary",)),
        name="gate_sums",
    )(gates)


def _mlstm_kernel(qkvo_ref, g_ref, c0_ref, n0_ref, m0_ref, hg_ref,
                  hm_ref, cf_ref, nf_ref, mf_ref, hf_s, hb_s, c_s, m_s, *, n_chunks):
    c_s[:, :, :HEAD_DIM, :] = c0_ref[0]
    c_s[:, :, HEAD_DIM:, :] = jnp.broadcast_to(n0_ref[0], (2, N_HEADS, STATE_PAD, HEAD_DIM))
    m_s[...] = m0_ref[0]
    src = lax.broadcasted_iota(I32, (CHUNK, CHUNK), 0)
    dst = lax.broadcasted_iota(I32, (CHUNK, CHUNK), 1)
    nt = (((1,), (1,)), ((), ()))

    zero_blk = jnp.zeros((CHUNK, HEAD_DIM), BF16)
    zero_pad = jnp.zeros((STATE_ROWS, HEAD_DIM), BF16)

    def diag2(x_a, x_b, zero):
        return jnp.concatenate([jnp.concatenate([x_a, zero], axis=1),
                                jnp.concatenate([zero, x_b], axis=1)], axis=0)

    def pair_direction(c, pair, d, prev):
        sl = pl.ds(pl.multiple_of(c * CHUNK, CHUNK), CHUNK)
        cols = lambda part: pl.ds((part * N_HEADS + 2 * pair) * HEAD_DIM, 2 * HEAD_DIM)
        q2 = qkvo_ref[0, sl, cols(0)]
        k2 = qkvo_ref[0, sl, cols(1)]
        vt2 = qkvo_ref[0, sl, cols(2)].astype(F32).T
        k_ab = [k2[:, :HEAD_DIM], k2[:, HEAD_DIM:]]
        vt_ab = [vt2[:HEAD_DIM, :], vt2[HEAD_DIM:, :]]
        mask = (src <= dst) if d == 0 else (src >= dst)
        rows = []
        for j in range(2):
            gates = g_ref[0, c, 2 * (2 * pair + j) + d]
            ig_row, b_row = gates[0:1, :], gates[1:2, :]
            gates_t = gates.T
            g = b_row[:, CHUNK - 1:CHUNK] if d == 0 else b_row[:, 0:1]
            w_log = g - b_row + ig_row
            m_loc = jnp.max(w_log, axis=1, keepdims=True)
            w = jnp.exp(w_log - m_loc)
            rows.append((ig_row, b_row, gates_t[:, 0:1], gates_t[:, 1:2], g, m_loc, w))
        c_bf = [prev[j][0].astype(BF16) for j in range(2)]
        lhs_q = jnp.concatenate([diag2(k_ab[0], k_ab[1], zero_blk), diag2(c_bf[0], c_bf[1], zero_pad)], axis=0)
        out_q = lax.dot_general(lhs_q, q2, nt, preferred_element_type=F32)
        s_list, extra = [], []
        for j in range(2):
            ig_row, b_row, ig_col, b_col, g, m_loc, w = rows[j]
            m_prev = prev[j][1]
            dmat = jnp.where(mask, b_row - b_col + ig_col, -jnp.inf)
            inter_log = b_row + m_prev
            m_comb = jnp.maximum(inter_log, jnp.max(dmat, axis=0, keepdims=True))
            s_t = out_q[j * CHUNK:(j + 1) * CHUNK, :] * jnp.exp(dmat - m_comb)
            s_list.append(s_t)
            extra.append((m_comb, jnp.exp(inter_log - m_comb)))
        s2 = jnp.concatenate(s_list, axis=0).astype(BF16)
        num2 = _dot(diag2(vt_ab[0].astype(BF16), vt_ab[1].astype(BF16), zero_blk), s2)
        lhs_k = [jnp.concatenate([vt_ab[j] * rows[j][6], jnp.broadcast_to(rows[j][6], (STATE_PAD, CHUNK))],
                                 axis=0).astype(BF16) for j in range(2)]
        k_rows = jnp.concatenate(k_ab, axis=0)
        c_loc2 = _dot(diag2(lhs_k[0], lhs_k[1], jnp.zeros((STATE_ROWS, CHUNK), BF16)), k_rows)
        res = []
        for j in range(2):
            ig_row, b_row, ig_col, b_col, g, m_loc, w = rows[j]
            c_prev, m_prev = prev[j]
            m_comb, w_inter = extra[j]
            cq = out_q[2 * CHUNK + j * STATE_ROWS:2 * CHUNK + (j + 1) * STATE_ROWS, :]
            num = num2[j * HEAD_DIM:(j + 1) * HEAD_DIM, :] + w_inter * cq[:HEAD_DIM, :]
            den = jnp.sum(s_list[j], axis=0, keepdims=True) + w_inter * cq[HEAD_DIM:HEAD_DIM + 1, :]
            h = num / jnp.maximum(jnp.abs(den), jnp.exp(-m_comb))
            m_new = jnp.maximum(g + m_prev, m_loc)
            a = jnp.exp(g + m_prev - m_new)
            bb = jnp.exp(m_loc - m_new)
            c_loc = c_loc2[j * STATE_ROWS:(j + 1) * STATE_ROWS, :]
            res.append((h, a * c_prev + bb * c_loc, m_new))
        return res

    def scan_body(i, carry):
        units = [(pair, d) for pair in range(N_HEADS // 2) for d in range(2)]
        prev = [[(c_s[d, 2 * pair + j], m_s[d, 2 * pair + j][:, 0:1]) for j in range(2)] for pair, d in units]
        chunk = [i, n_chunks - 1 - i]
        new = [pair_direction(chunk[d], pair, d, prev[u]) for u, (pair, d) in enumerate(units)]
        for (pair, d), res in zip(units, new):
            sl = pl.ds(pl.multiple_of(chunk[d] * CHUNK, CHUNK), CHUNK)
            for j, (h, c_new, m_new) in enumerate(res):
                head = 2 * pair + j
                (hf_s, hb_s)[d][head, :, sl] = h
                c_s[d, head] = c_new
                m_s[d, head] = jnp.broadcast_to(m_new, (1, LANES))
        return carry

    lax.fori_loop(0, n_chunks, scan_body, 0)

    def out_body(c, carry):
        sl = pl.ds(pl.multiple_of(c * CHUNK, CHUNK), CHUNK)
        for head in range(N_HEADS):
            hs = hf_s[head, :, sl] + hb_s[head, :, sl]
            hn = (hs * lax.rsqrt(jnp.mean(hs * hs, axis=0, keepdims=True) + EPS)).T * hg_ref[head]
            o = qkvo_ref[0, sl, pl.ds((3 * N_HEADS + head) * HEAD_DIM, HEAD_DIM)].astype(F32)
            hm_ref[0, sl, pl.ds(head * HEAD_DIM, HEAD_DIM)] = (hn * jax.nn.sigmoid(o)).astype(BF16)
        return carry

    lax.fori_loop(0, n_chunks, out_body, 0)
    cf_ref[0] = c_s[:, :, :HEAD_DIM, :]
    nf_ref[0] = c_s[:, :, HEAD_DIM:HEAD_DIM + 1, :]
    mf_ref[0] = m_s[...]


def _mlstm(qkvo3, gates5, c0, n0, m0, head_g3):
    b, t, w4 = qkvo3.shape
    nc = t // CHUNK
    once = dict(pipeline_mode=pl.Buffered(1)) if t * w4 * 2 > VMEM_LIMIT // 8 else {}
    st_c = pl.BlockSpec((1, 2, N_HEADS, HEAD_DIM, HEAD_DIM), lambda i: (i, 0, 0, 0, 0))
    st_v = pl.BlockSpec((1, 2, N_HEADS, 1, LANES), lambda i: (i, 0, 0, 0, 0))
    return pl.pallas_call(
        functools.partial(_mlstm_kernel, n_chunks=nc),
        grid=(b,),
        in_specs=[pl.BlockSpec((1, t, w4), lambda i: (i, 0, 0), **once),
                  pl.BlockSpec((1, nc, 2 * N_HEADS, GATE_ROWS, CHUNK), lambda i: (i, 0, 0, 0, 0), **once),
                  st_c, st_v, st_v,
                  pl.BlockSpec((N_HEADS, 1, HEAD_DIM), lambda i: (0, 0, 0))],
        out_specs=[pl.BlockSpec((1, t, MLSTM_WIDTH), lambda i: (i, 0, 0), **once), st_c, st_v, st_v],
        out_shape=[jax.ShapeDtypeStruct((b, t, MLSTM_WIDTH), BF16),
                   jax.ShapeDtypeStruct((b, 2, N_HEADS, HEAD_DIM, HEAD_DIM), F32),
                   jax.ShapeDtypeStruct((b, 2, N_HEADS, 1, LANES), F32),
                   jax.ShapeDtypeStruct((b, 2, N_HEADS, 1, LANES), F32)],
        scratch_shapes=[pltpu.VMEM((N_HEADS, HEAD_DIM, t), F32), pltpu.VMEM((N_HEADS, HEAD_DIM, t), F32),
                        pltpu.VMEM((2, N_HEADS, HEAD_DIM + STATE_PAD, HEAD_DIM), F32),
                        pltpu.VMEM((2, N_HEADS, 1, LANES), F32)],
        compiler_params=_cparams(("arbitrary",), VMEM_LIMIT),
        name="mlstm",
    )(qkvo3, gates5, c0, n0, m0, head_g3)


def _dft_cos_sin(n):
    k = np.arange(n)
    ang = 2.0 * np.pi * ((k[:, None] * k[None, :]) % n) / n
    return np.cos(ang), np.sin(ang)


def _channel_dft(scale):
    cd, sd = _dft_cos_sin(GROUP_DIM)
    eye = np.eye(N_GROUPS)
    return (jnp.asarray(np.kron(eye, cd) * scale, F32), jnp.asarray(np.kron(eye, -sd) * scale, F32))


def _fourier_ctx_kernel(u_ref, bdc_ref, bds_ref, ct_ref, st_ref, o_ref):
    u = u_ref[0]
    a = _dot3(u, bdc_ref[...])
    b = _dot3(u, bds_ref[...])
    o_ref[0] = (_dot3(ct_ref[...], a) + _dot3(st_ref[...], b)).astype(BF16)


def _fourier_ctx(u3):
    b, t, w = u3.shape
    bdc, bds = _channel_dft(1.0 / math.sqrt(t * GROUP_DIM))
    ct, st = _dft_cos_sin(t)
    const = lambda i: (0, 0)
    return pl.pallas_call(
        _fourier_ctx_kernel,
        grid=(b,),
        in_specs=[pl.BlockSpec((1, t, w), lambda i: (i, 0, 0)),
                  pl.BlockSpec((w, w), const), pl.BlockSpec((w, w), const),
                  pl.BlockSpec((t, t), const), pl.BlockSpec((t, t), const)],
        out_specs=pl.BlockSpec((1, t, w), lambda i: (i, 0, 0)),
        out_shape=jax.ShapeDtypeStruct((b, t, w), BF16),
        compiler_params=_cparams(("arbitrary",)),
        name="fourier_ctx",
    )(u3, bdc, bds, jnp.asarray(ct, F32), jnp.asarray(st, F32))


def _fourier_chan_kernel(u_ref, bdc_ref, bds_ref, a_ref, b_ref):
    u = u_ref[0].astype(BF16)
    a_ref[0] = _dot(u, bdc_ref[...]).astype(BF16)
    b_ref[0] = _dot(u, bds_ref[...]).astype(BF16)


def _fourier_pos_kernel(cre_ref, sre_ref, cct_ref, sct_ref, a_ref, b_ref, o_ref, cp_s, sp_s):
    @pl.when(pl.program_id(1) == 0)
    def _():
        gw = cct_ref.shape[0]
        for q in range(cre_ref.shape[0]):
            cr, sr = cre_ref[q], sre_ref[q]
            cc, sc = cct_ref[...], sct_ref[...]
            cp_s[q * gw:(q + 1) * gw, :] = (cr * cc - sr * sc).astype(BF16)
            sp_s[q * gw:(q + 1) * gw, :] = (sr * cc + cr * sc).astype(BF16)

    o_ref[0] = (_dot(cp_s[...], a_ref[0]) + _dot(sp_s[...], b_ref[0])).astype(BF16)


def _fourier_lat(u3, grid_w):
    b, t, w = u3.shape
    rows = t // grid_w
    bdc, bds = _channel_dft(1.0 / math.sqrt(t * GROUP_DIM))
    tm = 512
    const = lambda i, j: (0, 0)
    a, bm = pl.pallas_call(
        _fourier_chan_kernel,
        grid=(b, t // tm),
        in_specs=[pl.BlockSpec((1, tm, w), lambda i, j: (i, j, 0)),
                  pl.BlockSpec((w, w), const), pl.BlockSpec((w, w), const)],
        out_specs=[pl.BlockSpec((1, tm, w), lambda i, j: (i, j, 0))] * 2,
        out_shape=[jax.ShapeDtypeStruct((b, t, w), BF16)] * 2,
        compiler_params=_cparams(("arbitrary", "arbitrary")),
        name="fourier_chan",
    )(u3, bdc.astype(BF16), bds.astype(BF16))
    cr, sr = _dft_cos_sin(rows)
    cc, sc = _dft_cos_sin(grid_w)
    cre = jnp.asarray(np.repeat(cr, grid_w, axis=1)[:, None, :], F32)
    sre = jnp.asarray(np.repeat(sr, grid_w, axis=1)[:, None, :], F32)
    cct = jnp.asarray(np.tile(cc, (1, rows)), F32)
    sct = jnp.asarray(np.tile(sc, (1, rows)), F32)
    rpt = tm // grid_w
    return pl.pallas_call(
        _fourier_pos_kernel,
        grid=(t // tm, b),
        in_specs=[pl.BlockSpec((rpt, 1, t), lambda j, i: (j, 0, 0)),
                  pl.BlockSpec((rpt, 1, t), lambda j, i: (j, 0, 0)),
                  pl.BlockSpec((grid_w, t), lambda j, i: (0, 0)),
                  pl.BlockSpec((grid_w, t), lambda j, i: (0, 0)),
                  pl.BlockSpec((1, t, w), lambda j, i: (i, 0, 0)),
                  pl.BlockSpec((1, t, w), lambda j, i: (i, 0, 0))],
        out_specs=pl.BlockSpec((1, tm, w), lambda j, i: (i, j, 0)),
        out_shape=jax.ShapeDtypeStruct((b, t, w), BF16),
        scratch_shapes=[pltpu.VMEM((tm, t), BF16), pltpu.VMEM((tm, t), BF16)],
        compiler_params=_cparams(("arbitrary", "arbitrary"), VMEM_LIMIT),
        name="fourier_pos",
    )(cre, sre, cct, sct, a, bm)


def _out_kernel(hmc_ref, fuc_ref, xc_ref, hml_ref, ful_ref, xl_ref, mod_ref, g_ref, wa_ref, wb_ref, rw_ref,
                x1_ref, aug_ref, afft_ref, *, ctx_tiles):
    def tile(hm_ref, fu_ref, x_ref):
        a = _dot(hm_ref[...], wa_ref[...]) + _dot(fu_ref[...], wb_ref[...])
        x1 = x_ref[...] + mod_ref[0, 2:3, :] * a
        x1_ref[...] = x1
        ms = jnp.mean(x1 * x1, axis=-1, keepdims=True)
        y = x1 * lax.rsqrt(ms + EPS) * g_ref[...]
        h2 = y * (1.0 + mod_ref[0, 4:5, :]) + mod_ref[0, 3:4, :]
        h_hi, h_lo, _ = _split3(h2)
        w_hi, w_lo, _ = _split3(rw_ref[...])
        both = _dot(h_hi, jnp.concatenate([w_hi, w_lo], axis=1))
        logits = both[:, :LANES] + (both[:, LANES:] + _dot(h_lo, w_hi))
        lane = lax.broadcasted_iota(I32, logits.shape, 1)
        valid = lane < N_EXPERTS
        lg = jnp.where(valid, logits, -1e30)
        ex = jnp.where(valid, jnp.exp(lg - jnp.max(lg, axis=1, keepdims=True)), 0.0)
        aff = ex / jnp.sum(ex, axis=1, keepdims=True)
        aug_ref[:, :D_MODEL] = h2
        aug_ref[:, D_MODEL:] = aff
        afft_ref[...] = aff.T[:N_EXPERTS, :]

    @pl.when(pl.program_id(0) < ctx_tiles)
    def _():
        tile(hmc_ref, fuc_ref, xc_ref)

    @pl.when(pl.program_id(0) >= ctx_tiles)
    def _():
        tile(hml_ref, ful_ref, xl_ref)


def _out_proj(ctx, lat, mod, lat_tiles_per_batch, tm, g2, wa, wb, rw):
    n_c, n_l = ctx[2].shape[0], lat[2].shape[0]
    n = n_c + n_l
    ct = n_c // tm
    const = lambda i: (0, 0)
    ctx_tile = lambda i: (jnp.minimum(i, ct - 1), 0)
    lat_tile = lambda i: (jnp.maximum(i - ct, 0), 0)
    widths = (MLSTM_WIDTH, FOURIER_WIDTH, D_MODEL)
    return pl.pallas_call(
        functools.partial(_out_kernel, ctx_tiles=ct),
        grid=(n // tm,),
        in_specs=[pl.BlockSpec((tm, w), ctx_tile) for w in widths]
                 + [pl.BlockSpec((tm, w), lat_tile) for w in widths]
                 + [pl.BlockSpec((1, N_MOD, D_MODEL),
                                 lambda i: (jnp.where(i < ct, 0, 1 + (i - ct) // lat_tiles_per_batch), 0, 0)),
                    pl.BlockSpec((1, D_MODEL), const),
                    pl.BlockSpec((MLSTM_WIDTH, D_MODEL), const),
                    pl.BlockSpec((FOURIER_WIDTH, D_MODEL), const),
                    pl.BlockSpec((D_MODEL, LANES), const)],
        out_specs=[pl.BlockSpec((tm, D_MODEL), lambda i: (i, 0)),
                   pl.BlockSpec((tm, AUG), lambda i: (i, 0)),
                   pl.BlockSpec((N_EXPERTS, tm), lambda i: (0, i))],
        out_shape=[jax.ShapeDtypeStruct((n, D_MODEL), F32),
                   jax.ShapeDtypeStruct((n, AUG), F32),
                   jax.ShapeDtypeStruct((N_EXPERTS, n), F32)],
        compiler_params=_cparams(("arbitrary",), VMEM_LIMIT),
        name="out_proj",
    )(*ctx, *lat, mod, g2, wa, wb, rw)


def _topk_kernel(aff_ref, pos_ref, blk_ref, rng_ref, *, n, cap):
    nb = n // LANES
    aff = aff_ref[...]

    def enough(t):
        return jnp.sum(jnp.where(aff >= t, 1.0, 0.0), axis=1, keepdims=True) >= cap

    def pow2(k):
        return lax.bitcast_convert_type(jnp.left_shift(127 - k, 23), F32)

    def exp_search(_, c):
        lo, hi = c
        mid = jnp.right_shift(lo + hi, 1)
        ok = enough(pow2(jnp.minimum(mid, 126)))
        return jnp.where(ok, lo, mid + 1), jnp.where(ok, mid, hi)

    kz = jnp.zeros((N_EXPERTS, 1), I32)
    kstar, _ = lax.fori_loop(0, 7, exp_search, (kz, kz + 127))
    found = kstar < 127
    p = pow2(jnp.minimum(kstar, 126))
    t_lo0 = jnp.where(found, p, 0.0)
    t_hi0 = jnp.where(found, 2.0 * p, p)

    def bisect(_, c):
        t_lo, t_hi = c
        mid = 0.5 * (t_lo + t_hi)
        ok = enough(mid)
        return jnp.where(ok, mid, t_lo), jnp.where(ok, t_hi, mid)

    t_lo, t_hi = lax.fori_loop(0, 32, bisect, (t_lo0, t_hi0))
    n_gt = jnp.sum(jnp.where(aff >= t_hi, 1.0, 0.0), axis=1, keepdims=True)
    need = cap - n_gt
    row = lax.broadcasted_iota(I32, (LANES, LANES), 0)
    col = lax.broadcasted_iota(I32, (LANES, LANES), 1)
    upper = (row <= col).astype(BF16)
    blane = lax.broadcasted_iota(I32, blk_ref.shape, 1)

    group = 8

    def blocks(gi, carry):
        eq_off, sel_off = carry
        sls = [pl.ds(pl.multiple_of((gi * group + u) * LANES, LANES), LANES) for u in range(group)]
        abs_ = [aff_ref[:, sl] for sl in sls]
        eqf = [jnp.where((ab >= t_lo) & (ab < t_hi), 1.0, 0.0) for ab in abs_]
        eq_cum = [_dot(x.astype(BF16), upper) for x in eqf]
        self_ = []
        for u in range(group):
            eq_rank = eq_cum[u] + eq_off - eqf[u]
            sel = (abs_[u] >= t_hi) | ((eqf[u] > 0.0) & (eq_rank < need))
            self_.append(jnp.where(sel, 1.0, 0.0))
            eq_off = eq_off + jnp.sum(eqf[u], axis=1, keepdims=True)
        cum = [_dot(x.astype(BF16), upper) for x in self_]
        blk = blk_ref[...]
        for u in range(group):
            pos_ref[:, sls[u]] = jnp.where(self_[u] > 0.0, cum[u] + sel_off - 1.0, -1.0).astype(I32)
            blk = jnp.where(blane == gi * group + u, sel_off.astype(I32), blk)
            sel_off = sel_off + jnp.sum(self_[u], axis=1, keepdims=True)
        blk_ref[...] = blk
        return eq_off, sel_off

    blk_ref[...] = jnp.full(blk_ref.shape, cap, I32)
    zero = jnp.zeros((N_EXPERTS, 1), F32)
    lax.fori_loop(0, nb // group, blocks, (zero, zero))

    blk = blk_ref[...]
    rlane = lax.broadcasted_iota(I32, rng_ref.shape, 1)
    rng = jnp.zeros(rng_ref.shape, I32)
    for jc in range(cap // LANES):
        j0 = jc * LANES
        ends_before = jnp.where((blane >= 1) & (blane <= nb) & (blk <= j0), 1.0, 0.0)
        starts_below = jnp.where((blane < nb) & (blk < j0 + LANES), 1.0, 0.0)
        rng = jnp.where(rlane == jc, jnp.sum(ends_before, axis=1, keepdims=True).astype(I32), rng)
        rng = jnp.where(rlane == RANGE_HALF + jc, jnp.sum(starts_below, axis=1, keepdims=True).astype(I32), rng)
    rng_ref[...] = rng


def _topk(aff_t, cap):
    n = aff_t.shape[1]
    assert cap // LANES <= RANGE_HALF and n % (8 * LANES) == 0
    return pl.pallas_call(
        functools.partial(_topk_kernel, n=n, cap=cap),
        grid=(1,),
        in_specs=[pl.BlockSpec((N_EXPERTS, n), lambda i: (0, 0))],
        out_specs=[pl.BlockSpec((N_EXPERTS, n), lambda i: (0, 0)),
                   pl.BlockSpec((N_EXPERTS, 2 * LANES), lambda i: (0, 0)),
                   pl.BlockSpec((N_EXPERTS, LANES), lambda i: (0, 0))],
        out_shape=[jax.ShapeDtypeStruct((N_EXPERTS, n), I32),
                   jax.ShapeDtypeStruct((N_EXPERTS, 2 * LANES), I32),
                   jax.ShapeDtypeStruct((N_EXPERTS, LANES), I32)],
        compiler_params=_cparams(("arbitrary",)),
        name="topk",
    )(aff_t)


def _compact_kernel(rng_sm, pos_ref, out_ref):
    e = pl.program_id(0)
    slot0 = lax.broadcasted_iota(I32, (LANES, LANES), 0)
    tok = lax.broadcasted_iota(I32, (LANES, LANES), 1)

    def chunk(jc, carry):
        slot = slot0 + jc * LANES

        def body(b, acc):
            prow = pos_ref[0, :, pl.ds(pl.multiple_of(b * LANES, LANES), LANES)]
            return acc + jnp.where(prow == slot, (tok + b * LANES).astype(F32), 0.0)

        acc = lax.fori_loop(rng_sm[e, jc], rng_sm[e, RANGE_HALF + jc], body, jnp.zeros((LANES, LANES), F32))
        out_ref[0, jc] = jnp.sum(acc.T, axis=0, keepdims=True).astype(I32)
        return carry

    lax.fori_loop(0, out_ref.shape[1], chunk, 0)


def _compact(pos, rng, cap):
    n = pos.shape[1]
    ncj = cap // LANES
    out = pl.pallas_call(
        _compact_kernel,
        grid_spec=pltpu.PrefetchScalarGridSpec(
            num_scalar_prefetch=1,
            grid=(N_EXPERTS,),
            in_specs=[pl.BlockSpec((1, 1, n), lambda e, rng: (e, 0, 0))],
            out_specs=pl.BlockSpec((1, ncj, 1, LANES), lambda e, rng: (e, 0, 0, 0)),
        ),
        out_shape=jax.ShapeDtypeStruct((N_EXPERTS, ncj, 1, LANES), I32),
        compiler_params=_cparams(("arbitrary",)),
        name="compact",
    )(rng, pos.reshape(N_EXPERTS, 1, n))
    return out.reshape(N_EXPERTS, cap)


def _moe_kernel(lists_sm, aug, w1_ref, w3_ref, w2_ref, ye_ref,
                xbuf, xe, gate, act_all, w2_all, w1b, w3b, sem, *, rows, row_chunk, unroll, tf, nf_static):
    e = pl.program_id(0)
    f = pl.program_id(1)
    ne = pl.num_programs(0)
    nf = pl.num_programs(1)

    def start_row(list_pos, j):
        pltpu.make_async_copy(aug.at[pl.ds(lists_sm[list_pos], 1)], xbuf.at[pl.ds(j, 1)], sem.at[0]).start()

    def wait_rows():
        pltpu.make_async_copy(xbuf, xbuf, sem.at[0]).wait()

    @pl.when((e == 0) & (f == 0))
    def _():
        def body(g, carry):
            for u in range(unroll):
                start_row(g * unroll + u, g * unroll + u)
            return carry

        lax.fori_loop(0, rows // unroll, body, 0)

    @pl.when(f == 0)
    def _():
        wait_rows()
        xe[...] = xbuf[:, :D_MODEL].astype(BF16)
        lane = lax.broadcasted_iota(I32, (rows, LANES), 1)
        gate[...] = jnp.sum(jnp.where(lane == e, xbuf[:, D_MODEL:], 0.0), axis=1, keepdims=True)
        ye_ref[0, rows:, :] = jnp.zeros((ye_ref.shape[1] - rows, D_MODEL), BF16)

    w1b[...] = w1_ref[0].astype(BF16)
    w3b[...] = w3_ref[0].astype(BF16)
    fsl = pl.ds(pl.multiple_of(f * tf, tf), tf)
    w2_all[fsl, :] = w2_ref[0].astype(BF16)
    n_chunks = rows // row_chunk
    up_rows = rows // 4
    up_share = up_rows // (nf_static * n_chunks)
    down_share = (rows - up_rows) // n_chunks
    nxt = (e + 1) * rows
    for r in range(n_chunks):
        sl = pl.ds(r * row_chunk, row_chunk)
        x = xe[sl, :]
        a = _dot(x, w1b[...])
        bg = _dot(x, w3b[...])
        act_all[sl, fsl] = (a * jax.nn.sigmoid(a) * bg).astype(BF16)
        for u in range(up_share):
            j = (f * n_chunks + r) * up_share + u
            start_row(nxt + j, j)

    @pl.when(f == nf - 1)
    def _():
        for r in range(n_chunks):
            sl = pl.ds(r * row_chunk, row_chunk)
            ye_ref[0, sl, :] = (_dot(act_all[sl, :], w2_all[...]) * gate[sl, :]).astype(BF16)
            for u in range(down_share):
                j = up_rows + r * down_share + u
                start_row(nxt + j, j)

        @pl.when(e == ne - 1)
        def _():
            wait_rows()


def _moe(lists, aug, w1, w3, w2, rows, tf=256, row_chunk=512, unroll=8):
    nf = D_EXPERT // tf
    assert rows % (4 * nf * (rows // row_chunk)) == 0 and rows % unroll == 0
    return pl.pallas_call(
        functools.partial(_moe_kernel, rows=rows, row_chunk=row_chunk, unroll=unroll, tf=tf, nf_static=nf),
        grid_spec=pltpu.PrefetchScalarGridSpec(
            num_scalar_prefetch=1,
            grid=(N_EXPERTS, nf),
            in_specs=[pl.BlockSpec(memory_space=pl.ANY),
                      pl.BlockSpec((1, D_MODEL, tf), lambda e, f, ls: (e, 0, f)),
                      pl.BlockSpec((1, D_MODEL, tf), lambda e, f, ls: (e, 0, f)),
                      pl.BlockSpec((1, tf, D_MODEL), lambda e, f, ls: (e, f, 0))],
            out_specs=pl.BlockSpec((1, rows + WINDOW, D_MODEL), lambda e, f, ls: (e, 0, 0)),
            scratch_shapes=[pltpu.VMEM((rows, AUG), F32), pltpu.VMEM((rows, D_MODEL), BF16),
                            pltpu.VMEM((rows, 1), F32),
                            pltpu.VMEM((rows, D_EXPERT), BF16), pltpu.VMEM((D_EXPERT, D_MODEL), BF16),
                            pltpu.VMEM((D_MODEL, tf), BF16), pltpu.VMEM((D_MODEL, tf), BF16),
                            pltpu.SemaphoreType.DMA((1,))],
        ),
        out_shape=jax.ShapeDtypeStruct((N_EXPERTS, rows + WINDOW, D_MODEL), BF16),
        compiler_params=_cparams(("arbitrary", "arbitrary"), VMEM_LIMIT),
        name="moe",
    )(lists, aug, w1, w3, w2)


def _combine_kernel(seg_sm, x1_ref, pos_ref, mod_ref, g_ref, ye_hbm, o_ref, wbuf, y_s, sem, *, base, tm):
    i = pl.program_id(0)
    nt = pl.num_programs(0)
    bpt = tm // LANES
    slot = lax.rem(i, 2)
    last_start = ye_hbm.shape[1] - WINDOW
    tn = (((0,), (0,)), ((), ()))

    def start_windows(tile, rnd, buf):
        for e in range(N_EXPERTS):
            start = jnp.minimum(base + (seg_sm[e, tile * bpt] & -ROW_ALIGN) + rnd * WINDOW, last_start)
            pltpu.make_async_copy(ye_hbm.at[e, pl.ds(pl.multiple_of(start, ROW_ALIGN), WINDOW)],
                                  wbuf.at[buf, pl.ds(e * WINDOW, WINDOW)], sem.at[buf]).start()

    def wait_windows(buf):
        pltpu.make_async_copy(wbuf.at[buf], wbuf.at[buf], sem.at[buf]).wait()

    @pl.when(i == 0)
    def _():
        start_windows(i, 0, slot)

    most = jnp.int32(0)
    for e in range(N_EXPERTS):
        most = jnp.maximum(most, seg_sm[e, (i + 1) * bpt] - (seg_sm[e, i * bpt] & -ROW_ALIGN))
    n_rounds = jnp.right_shift(most + (WINDOW - 1), WINDOW_SHIFT)

    pos = pos_ref[...]
    valid = pos >= 0
    first = jnp.min(jnp.where(valid, pos, jnp.int32(2 ** 30)), axis=1, keepdims=True)
    rel = pos - (first & -ROW_ALIGN)
    lane = lax.broadcasted_iota(I32, (N_EXPERTS, N_EXPERTS * WINDOW), 1)
    owner = lax.broadcasted_iota(I32, (N_EXPERTS, N_EXPERTS * WINDOW), 0)
    spread = jnp.where(jnp.right_shift(lane, WINDOW_SHIFT) == owner, 1.0, 0.0).astype(BF16)
    wrow = (lax.broadcasted_iota(I32, (1, N_EXPERTS * WINDOW), 1) & (WINDOW - 1)).astype(F32)

    def place(rnd, first_round):
        q = jnp.where(valid & (rel >= rnd * WINDOW) & (rel < (rnd + 1) * WINDOW), rel - rnd * WINDOW, -1)
        q = q.astype(F32).astype(BF16)
        for part in range(tm // LANES):
            tok = pl.ds(part * LANES, LANES)
            qb = lax.dot_general(q[:, part * LANES:(part + 1) * LANES], spread, tn, preferred_element_type=F32)
            onehot = jnp.where(qb == wrow, 1.0, 0.0).astype(BF16)
            rows_y = _dot(onehot, wbuf[slot])
            if first_round:
                y_s[tok, :] = rows_y
            else:
                y_s[tok, :] += rows_y

    wait_windows(slot)
    start_windows(jnp.minimum(i + 1, nt - 1), 0, 1 - slot)
    place(0, True)

    def extra_round(rnd, carry):
        start_windows(i, rnd, slot)
        wait_windows(slot)
        place(rnd, False)
        return carry

    lax.fori_loop(1, n_rounds, extra_round, 0)
    x = x1_ref[...] + mod_ref[0, 5:6, :] * y_s[...]
    ms = jnp.mean(x * x, axis=-1, keepdims=True)
    o_ref[...] = x * lax.rsqrt(ms + EPS) * g_ref[...]

    @pl.when(i == nt - 1)
    def _():
        wait_windows(1 - slot)


def _combine_norm(seg, x1, first_row, pos, ye, base, mod, row0, tiles_per_batch, tm, gf):
    n = pos.shape[1]
    tile0 = first_row // tm
    return pl.pallas_call(
        functools.partial(_combine_kernel, base=base, tm=tm),
        grid_spec=pltpu.PrefetchScalarGridSpec(
            num_scalar_prefetch=1,
            grid=(n // tm,),
            in_specs=[pl.BlockSpec((tm, D_MODEL), lambda i, sg: (i + tile0, 0)),
                      pl.BlockSpec((N_EXPERTS, tm), lambda i, sg: (0, i)),
                      pl.BlockSpec((1, N_MOD, D_MODEL),
                                   lambda i, sg: (row0 + i // tiles_per_batch, 0, 0)),
                      pl.BlockSpec((1, D_MODEL), lambda i, sg: (0, 0)),
                      pl.BlockSpec(memory_space=pl.ANY)],
            out_specs=pl.BlockSpec((tm, D_MODEL), lambda i, sg: (i, 0)),
            scratch_shapes=[pltpu.VMEM((2, N_EXPERTS * WINDOW, D_MODEL), BF16),
                            pltpu.VMEM((tm, D_MODEL), F32),
                            pltpu.SemaphoreType.DMA((2,))],
        ),
        out_shape=jax.ShapeDtypeStruct((n, D_MODEL), F32),
        compiler_params=_cparams(("arbitrary",), VMEM_LIMIT),
        name="combine_norm",
    )(seg, x1, pos, mod, gf, ye)


def kernel(x_prompt, x_sample, state_C, state_n, state_m, c, c_ctx, ada_w, ada_b, norm1_g, norm2_g,
           w_in, b_in, head_g, w_out, router_w, exp_w1, exp_w3, exp_w2, final_g):
    depth = ada_w.shape[0]
    assert depth == 1, "single-layer trunk"
    bc_, tc_, _ = x_prompt.shape
    bl_, tl_, _ = x_sample.shape
    w4 = 4 * MLSTM_WIDTH
    tm = 512

    cvec = jnp.concatenate([c_ctx[None, :], c, jnp.zeros((8 - 1 - bl_, D_MODEL), F32)], axis=0)
    mod = _modulation(cvec, ada_w[0], ada_b[0])

    wi = w_in[0]
    bi = b_in[0]
    wq = wi[:, :w4].astype(BF16)
    bq = bi[None, :w4]
    wg = jnp.pad(wi[:, w4:w4 + 16], ((0, 0), (0, LANES - 16))).astype(BF16)
    bg = jnp.pad(bi[w4:w4 + 16], (0, LANES - 16))[None, :]
    wf = wi[:, w4 + 16:].astype(BF16)
    bfo = bi[None, w4 + 16:]
    colscale = jnp.ones((1, w4), F32).at[:, MLSTM_WIDTH:2 * MLSTM_WIDTH].set(HEAD_DIM ** -0.5)
    g1 = norm1_g[0][None, :]
    g2 = norm2_g[0][None, :]
    wo = w_out[0].astype(BF16)
    wa, wb = wo[:MLSTM_WIDTH], wo[MLSTM_WIDTH:]
    rw = jnp.pad(router_w[0], ((0, 0), (0, LANES - N_EXPERTS)))
    hg3 = head_g[0][:, None, :]

    def spread(v):
        return jnp.broadcast_to(v[..., None, None], v.shape + (1, LANES))

    n_ctx, n_lat = bc_ * tc_, bl_ * tl_

    def mixer(x3, row0, init, grid_w):
        b, t, _ = x3.shape
        x2 = x3.reshape(b * t, D_MODEL)
        tpb = max(t // tm, 1) if row0 else (b * t) // tm + 1
        qkvo, gates, u = _in_proj(x2, mod, row0, tpb, tm, g1, wq, bq, colscale, wg, bg, wf, bfo)
        gate_rows = _gate_sums(gates).reshape(b, t // CHUNK, 2 * N_HEADS, GATE_ROWS, CHUNK)
        c0, n0, m0 = init
        hm, cf, nf, mf = _mlstm(qkvo.reshape(b, t, w4), gate_rows, c0, n0, m0, hg3)
        u3 = u.reshape(b, t, FOURIER_WIDTH)
        fu = _fourier_ctx(u3) if grid_w is None else _fourier_lat(u3, grid_w)
        return (hm.reshape(b * t, MLSTM_WIDTH), fu.reshape(b * t, FOURIER_WIDTH), x2), (cf, nf, mf)

    zero_init = (jnp.zeros((bc_, 2, N_HEADS, HEAD_DIM, HEAD_DIM), F32),
                 jnp.zeros((bc_, 2, N_HEADS, 1, LANES), F32),
                 jnp.zeros((bc_, 2, N_HEADS, 1, LANES), F32))
    lat_init = (state_C[:, 0], state_n[:, 0][..., None, :], spread(state_m[:, 0]))
    ctx_parts, (cf, nf, mf) = mixer(x_prompt, 0, zero_init, None)
    lat_parts, _ = mixer(x_sample, 1, lat_init, GRID_W)
    tmo = 256
    x1, table, aff_t = _out_proj(ctx_parts, lat_parts, mod, tl_ // tmo, tmo, g2, wa, wb, rw)

    capc = CAPACITY_FACTOR * n_ctx // N_EXPERTS
    capl = CAPACITY_FACTOR * n_lat // N_EXPERTS
    posc, blkc, rngc = _topk(aff_t[:, :n_ctx], capc)
    posl, blkl, rngl = _topk(aff_t[:, n_ctx:], capl)
    lists = jnp.concatenate([_compact(posc, rngc, capc), _compact(posl, rngl, capl) + n_ctx], axis=1)
    lists = jnp.concatenate([lists, jnp.zeros((1, capc + capl), I32)], axis=0).reshape(-1)
    ye = _moe(lists, table, exp_w1[0], exp_w3[0], exp_w2[0], capc + capl)
    gf = final_g[None, :]
    tc = 256
    y_prompt = _combine_norm(blkc, x1, 0, posc, ye, 0, mod, 0, n_ctx // tc + 1, tc, gf)
    y_sample = _combine_norm(blkl, x1, n_ctx, posl, ye, capc, mod, 1, tl_ // tc, tc, gf)
    y_prompt = y_prompt.reshape(x_prompt.shape)
    y_sample = y_sample.reshape(x_sample.shape)
    new_c = cf[:, None]
    new_n = nf[:, None, :, :, 0, :]
    new_m = mf[:, None, :, :, 0, 0]
    return (y_prompt, y_sample, new_c, new_n, new_m)
```
